```python
import math
import jax, jax.numpy as jnp
from jax import lax
import numpy as np

D_MODEL = 1024
BATCH = 8
SEQ = 4096
DEPTH = 2

N_A_LAYERS = DEPTH // 2
N_B_LAYERS = DEPTH - N_A_LAYERS
CONV_WIDTH = 2 * D_MODEL
CONV_KERNEL = 31
N_HEADS = 16
QK_NOPE_DIM = 128
QK_ROPE_DIM = 64
V_HEAD_DIM = 128
KV_LORA_RANK = D_MODEL // 4
Q_LORA_RANK = D_MODEL // 2
ROPE_THETA = 10000.0
Q_BLOCK = 128
LN_EPS = 1e-5
RMS_EPS = 1e-6
MASK_VALUE = -1e30

kernel_name = "yoco_conformer_conv_mla_deepnorm"


def layer_norm(x, g, b):
    xf = x.astype(jnp.float32)
    mu = jnp.mean(xf, axis=-1, keepdims=True)
    var = jnp.mean(jnp.square(xf - mu), axis=-1, keepdims=True)
    return ((xf - mu) * lax.rsqrt(var + LN_EPS)).astype(x.dtype) * g + b


def rms_norm(x, g):
    xf = x.astype(jnp.float32)
    ms = jnp.mean(jnp.square(xf), axis=-1, keepdims=True)
    return (xf * lax.rsqrt(ms + RMS_EPS)).astype(x.dtype) * g


def rope_tables(positions, dtype):
    freqs = ROPE_THETA ** (-jnp.arange(0, QK_ROPE_DIM, 2, dtype=jnp.float32) / QK_ROPE_DIM)
    ang = positions.astype(jnp.float32)[..., None] * freqs
    return jnp.cos(ang).astype(dtype), jnp.sin(ang).astype(dtype)


def apply_rope(x, cos, sin):
    x1, x2 = jnp.split(x, 2, axis=-1)
    return jnp.concatenate([x1 * cos - x2 * sin, x2 * cos + x1 * sin], axis=-1)


def conformer_conv_branch(h, w_in, b_in, conv_w, conv_b, norm_g, norm_b, w_out, b_out):
    proj = h @ w_in + b_in
    val, glu_gate, z = jnp.split(proj, 3, axis=-1)
    u = val * jax.nn.sigmoid(glu_gate)
    u = lax.conv_general_dilated(
        u, conv_w[:, None, :], window_strides=(1,), padding=[(CONV_KERNEL - 1, 0)],
        dimension_numbers=('NWC', 'WIO', 'NWC'), feature_group_count=CONV_WIDTH) + conv_b
    u = jax.nn.silu(layer_norm(u, norm_g, norm_b))
    u = u * jax.nn.silu(z)
    return u @ w_out + b_out


def shared_latent_kv(h, w_down, kv_norm_g, w_uk, w_uv, cos, sin):
    ckv_kr = h @ w_down
    c_kv, k_rope = jnp.split(ckv_kr, [KV_LORA_RANK], axis=-1)
    c_kv = rms_norm(c_kv, kv_norm_g)
    k_nope = jnp.einsum('bsr,rhd->bshd', c_kv, w_uk)
    v = jnp.einsum('bsr,rhd->bshd', c_kv, w_uv)
    k_rope = apply_rope(k_rope, cos, sin)
    return k_nope, k_rope, v


def causal_mla_attention(q_nope, q_rope, k_nope, k_rope, v):
    scale = 1.0 / math.sqrt(QK_NOPE_DIM + QK_ROPE_DIM)
    seq = q_nope.shape[1]
    outs = []
    for blk in range(seq // Q_BLOCK):
        q0 = blk * Q_BLOCK
        kend = q0 + Q_BLOCK
        s = (jnp.einsum('bqhd,bkhd->bhqk', q_nope[:, q0:kend], k_nope[:, :kend])
             + jnp.einsum('bqhr,bkr->bhqk', q_rope[:, q0:kend], k_rope[:, :kend]))
        s = s.astype(jnp.float32) * scale
        mask = jnp.arange(kend)[None, :] <= (q0 + jnp.arange(Q_BLOCK))[:, None]
        s = jnp.where(mask, s, MASK_VALUE)
        p = jax.nn.softmax(s, axis=-1).astype(v.dtype)
        outs.append(jnp.einsum('bhqk,bkhd->bqhd', p, v[:, :kend]))
    return jnp.concatenate(outs, axis=1)


def mla_branch(h, k_nope, k_rope, v, w_in, q_norm_g, w_uq, w_out, cos, sin):
    bsz, seq, _ = h.shape
    proj = h @ w_in
    c_q, z = jnp.split(proj, [Q_LORA_RANK], axis=-1)
    q = jnp.einsum('bsr,rhd->bshd', rms_norm(c_q, q_norm_g), w_uq)
    q_nope, q_rope = jnp.split(q, [QK_NOPE_DIM], axis=-1)
    q_rope = apply_rope(q_rope, cos[:, :, None, :], sin[:, :, None, :])
    o = causal_mla_attention(q_nope, q_rope, k_nope, k_rope, v)
    o = o.reshape(bsz, seq, N_HEADS * V_HEAD_DIM) * jax.nn.silu(z)
    return o @ w_out


def setup_inputs(seed: int = 0) -> dict:
    key = jax.random.key(seed)
    ks = jax.random.split(key, 20)
    f32 = jnp.float32
    beta = (8.0 * DEPTH) ** -0.25
    E = CONV_WIDTH
    HV = N_HEADS * V_HEAD_DIM

    def dense(k, shape, fan_in, scale=1.0):
        return jax.random.normal(k, shape, f32) * (scale * fan_in ** -0.5)

    def gain(k, shape):
        return 1.0 + 0.02 * jax.random.normal(k, shape, f32)

    def small(k, shape):
        return 0.02 * jax.random.normal(k, shape, f32)

    return {
        "x": jax.random.normal(ks[0], (BATCH, SEQ, D_MODEL), f32),
        "positions": jnp.broadcast_to(jnp.arange(SEQ, dtype=jnp.int32), (BATCH, SEQ)),
        "ln_g": gain(ks[1], (DEPTH, D_MODEL)),
        "ln_b": small(ks[2], (DEPTH, D_MODEL)),
        "a_w_in": dense(ks[3], (N_A_LAYERS, D_MODEL, 3 * E), D_MODEL),
        "a_b_in": small(ks[4], (N_A_LAYERS, 3 * E)),
        "a_conv_w": dense(ks[5], (N_A_LAYERS, CONV_KERNEL, E), CONV_KERNEL),
        "a_conv_b": small(ks[6], (N_A_LAYERS, E)),
        "a_norm_g": gain(ks[7], (N_A_LAYERS, E)),
        "a_norm_b": small(ks[8], (N_A_LAYERS, E)),
        "a_w_out": dense(ks[9], (N_A_LAYERS, E, D_MODEL), E, beta),
        "a_b_out": small(ks[10], (N_A_LAYERS, D_MODEL)),
        "kv_w_down": dense(ks[11], (D_MODEL, KV_LORA_RANK + QK_ROPE_DIM), D_MODEL),
        "kv_norm_g": gain(ks[12], (KV_LORA_RANK,)),
        "kv_w_uk": dense(ks[13], (KV_LORA_RANK, N_HEADS, QK_NOPE_DIM), KV_LORA_RANK),
        "kv_w_uv": dense(ks[14], (KV_LORA_RANK, N_HEADS, V_HEAD_DIM), KV_LORA_RANK),
        "b_w_in": dense(ks[15], (N_B_LAYERS, D_MODEL, Q_LORA_RANK + HV), D_MODEL),
        "b_q_norm_g": gain(ks[16], (N_B_LAYERS, Q_LORA_RANK)),
        "b_w_uq": dense(ks[17], (N_B_LAYERS, Q_LORA_RANK, N_HEADS, QK_NOPE_DIM + QK_ROPE_DIM), Q_LORA_RANK),
        "b_w_out": dense(ks[18], (N_B_LAYERS, HV, D_MODEL), HV, beta),
    }


def reference(x, positions, ln_g, ln_b, a_w_in, a_b_in, a_conv_w, a_conv_b, a_norm_g, a_norm_b,
              a_w_out, a_b_out, kv_w_down, kv_norm_g, kv_w_uk, kv_w_uv, b_w_in, b_q_norm_g,
              b_w_uq, b_w_out):
    alpha = (2.0 * DEPTH) ** 0.25
    cos, sin = rope_tables(positions, x.dtype)
    h = x
    k_nope = k_rope = v = None
    for layer in range(DEPTH):
        if layer < N_A_LAYERS:
            i = layer
            y = conformer_conv_branch(h, a_w_in[i], a_b_in[i], a_conv_w[i], a_conv_b[i],
                                      a_norm_g[i], a_norm_b[i], a_w_out[i], a_b_out[i])
        else:
            if layer == N_A_LAYERS:
                k_nope, k_rope, v = shared_latent_kv(h, kv_w_down, kv_norm_g, kv_w_uk, kv_w_uv, cos, sin)
            j = layer - N_A_LAYERS
            y = mla_branch(h, k_nope, k_rope, v, b_w_in[j], b_q_norm_g[j], b_w_uq[j], b_w_out[j], cos, sin)
        h = layer_norm(alpha * h + y, ln_g[layer], ln_b[layer])
    return h
```

```python
import functools
import math

import jax
import jax.numpy as jnp
from jax import lax
from jax.experimental import pallas as pl
from jax.experimental.pallas import tpu as pltpu

F32 = jnp.float32
BF16 = jnp.bfloat16

LN_EPS = 1e-5
RMS_EPS = 1e-6
ROPE_THETA = 10000.0
MASK_VALUE = -1e30
LOG2E = math.log2(math.e)

SUBLANES = 8
LANES = 128
VMEM_LIMIT_BYTES = 56 * 1024 * 1024

T_A = 256
CW_A = 512
T_P = 256
T_O = 256

_NT = (((1,), (1,)), ((), ()))


def _sigmoid(x):
    return jax.nn.sigmoid(x)


def _layer_norm_rows(h, g, b):
    mu = jnp.mean(h, axis=-1, keepdims=True)
    d = h - mu
    var = jnp.mean(d * d, axis=-1, keepdims=True)
    return d * lax.rsqrt(var + LN_EPS) * g + b


def _rms_norm_rows(x, g):
    ms = jnp.mean(x * x, axis=-1, keepdims=True)
    return x * lax.rsqrt(ms + RMS_EPS) * g


def _const_spec(shape):
    n = len(shape)
    return pl.BlockSpec(shape, lambda *_: (0,) * n, pipeline_mode=pl.Buffered(1))


def _layer_a_body(x_ref, w_in_ref, b_in_ref, cw_ref, cb_ref, ng_ref, nb_ref, w_out_ref,
                  b_out_ref, lg_ref, lb_ref, o_ref, ubuf, cbuf, zbuf, *, T, E, CW, KW, HALO,
                  alpha):
    @pl.when(pl.program_id(1) == 0)
    def _():
        ubuf[0:HALO, :] = jnp.zeros((HALO, E), F32)

    x = x_ref[0]
    xb = x.astype(BF16)
    for c in range(E // CW):
        lo = c * CW

        def proj(off):
            w = w_in_ref[:, off + lo:off + lo + CW]
            return (jnp.dot(xb, w, preferred_element_type=F32)
                    + b_in_ref[:, off + lo:off + lo + CW])

        val, gate, z = proj(0), proj(E), proj(2 * E)
        ubuf[HALO:HALO + T, lo:lo + CW] = val * _sigmoid(gate)
        zbuf[:, lo:lo + CW] = z * _sigmoid(z)
        acc = jnp.zeros((T, CW), F32)
        for k in range(KW):
            o = HALO - (KW - 1) + k
            acc = acc + ubuf[o:o + T, lo:lo + CW] * cw_ref[k:k + 1, lo:lo + CW]
        cbuf[:, lo:lo + CW] = acc + cb_ref[:, lo:lo + CW]
    ubuf[0:HALO, :] = ubuf[T:T + HALO, :]

    g = _layer_norm_rows(cbuf[...], ng_ref[...], nb_ref[...])
    a = (g * _sigmoid(g) * zbuf[...]).astype(BF16)
    y = jnp.dot(a, w_out_ref[...], preferred_element_type=F32) + b_out_ref[...]
    o_ref[0] = _layer_norm_rows(alpha * x + y, lg_ref[...], lb_ref[...])


def _layer_a(h, w_in, b_in, conv_w, conv_b, norm_g, norm_b, w_out, b_out, ln_g, ln_b, alpha):
    B, S, D = h.shape
    E = w_out.shape[0]
    KW = conv_w.shape[0]
    HALO = -(-(KW - 1) // SUBLANES) * SUBLANES
    T, CW = T_A, CW_A
    body = functools.partial(_layer_a_body, T=T, E=E, CW=CW, KW=KW, HALO=HALO, alpha=alpha)
    row = lambda v: v.reshape(1, -1).astype(F32)
    return pl.pallas_call(
        body,
        out_shape=jax.ShapeDtypeStruct((B, S, D), F32),
        grid=(B, S // T),
        in_specs=[
            pl.BlockSpec((1, T, D), lambda b, t: (b, t, 0)),
            _const_spec((D, 3 * E)), _const_spec((1, 3 * E)),
            _const_spec((KW, E)), _const_spec((1, E)),
            _const_spec((1, E)), _const_spec((1, E)),
            _const_spec((E, D)), _const_spec((1, D)),
            _const_spec((1, D)), _const_spec((1, D)),
        ],
        out_specs=pl.BlockSpec((1, T, D), lambda b, t: (b, t, 0)),
        scratch_shapes=[
            pltpu.VMEM((HALO + T, E), F32),
            pltpu.VMEM((T, E), F32),
            pltpu.VMEM((T, E), F32),
        ],
        compiler_params=pltpu.CompilerParams(
            dimension_semantics=("arbitrary", "arbitrary"),
            vmem_limit_bytes=VMEM_LIMIT_BYTES),
        name="layer_a",
    )(h, w_in.astype(BF16), row(b_in), conv_w.astype(F32), row(conv_b), row(norm_g),
      row(norm_b), w_out.astype(BF16), row(b_out), row(ln_g), row(ln_b))


def _mla_proj_body(h_ref, pos_ref, fr_ref, w_down_ref, kvg_ref, w_uk_ref, w_uvT_ref, w_drT_ref,
                   w_in_ref, qg_ref, w_uqT_ref, qT_ref, sz_ref, kn_ref, kr_ref, vT_ref, *,
                   H, RQ, DN, DR, DV, DQP, q_scale):
    hb = h_ref[0].astype(BF16)
    T = hb.shape[0]
    ang = fr_ref[...] * pos_ref[0].astype(F32)
    cosT, sinT = jnp.cos(ang), jnp.sin(ang)

    def rope_t(xT):
        x1, x2 = xT[:DR // 2], xT[DR // 2:]
        return x1 * cosT - x2 * sinT, x2 * cosT + x1 * sinT

    ckv = jnp.dot(hb, w_down_ref[...], preferred_element_type=F32)
    cn = _rms_norm_rows(ckv, kvg_ref[...]).astype(BF16)
    kn = jnp.dot(cn, w_uk_ref[...], preferred_element_type=F32)
    for hd in range(H):
        kn_ref[0, hd] = kn[:, hd * DN:(hd + 1) * DN].astype(BF16)
    vT = lax.dot_general(w_uvT_ref[...], cn, _NT, preferred_element_type=F32)
    for hd in range(H):
        vT_ref[0, hd, 0] = vT[hd * DV:(hd + 1) * DV].astype(BF16)
    krT = lax.dot_general(w_drT_ref[...], hb, _NT, preferred_element_type=F32)
    y1, y2 = rope_t(krT[:DR])
    krT = jnp.concatenate([y1, y2, jnp.zeros((LANES - DR, T), F32)], axis=0)
    kr_ref[0] = krT.T.astype(BF16)

    proj = jnp.dot(hb, w_in_ref[...], preferred_element_type=F32)
    z = proj[:, RQ:]
    sz_ref[0] = (z * _sigmoid(z)).astype(BF16)
    cqn = _rms_norm_rows(proj[:, :RQ], qg_ref[...]).astype(BF16)
    qT = lax.dot_general(w_uqT_ref[...], cqn, _NT, preferred_element_type=F32) * q_scale
    DQ = DN + DR
    for hd in range(H):
        base = hd * DQ
        qT_ref[0, hd, 0, 0:DN, :] = qT[base:base + DN].astype(BF16)
        y1, y2 = rope_t(qT[base + DN:base + DQ])
        qT_ref[0, hd, 0, DN:DN + DR // 2, :] = y1.astype(BF16)
        qT_ref[0, hd, 0, DN + DR // 2:DQ, :] = y2.astype(BF16)
        qT_ref[0, hd, 0, DQ:DQP, :] = jnp.zeros((DQP - DQ, T), BF16)


def _mla_proj(h, pos3, freqs, w_down_c, kv_norm_g, w_uk, w_uvT, w_drT, w_in, q_norm_g, w_uqT,
              *, H, DN, DR, DV, q_scale):
    B, S, D = h.shape
    T = T_P
    NT = S // T
    RKV = w_down_c.shape[1]
    RQ = q_norm_g.shape[-1]
    DQP = 2 * LANES
    body = functools.partial(_mla_proj_body, H=H, RQ=RQ, DN=DN, DR=DR, DV=DV, DQP=DQP,
                             q_scale=q_scale)
    return pl.pallas_call(
        body,
        out_shape=(
            jax.ShapeDtypeStruct((B, H, NT, DQP, T), BF16),
            jax.ShapeDtypeStruct((B, S, H * DV), BF16),
            jax.ShapeDtypeStruct((B, H, S, DN), BF16),
            jax.ShapeDtypeStruct((B, S, LANES), BF16),
            jax.ShapeDtypeStruct((B, H, NT, DV, T), BF16),
        ),
        grid=(B, NT),
        in_specs=[
            pl.BlockSpec((1, T, D), lambda b, t: (b, t, 0)),
            pl.BlockSpec((1, 1, T), lambda b, t: (b, 0, t)),
            _const_spec((DR // 2, 1)),
            _const_spec((D, RKV)), _const_spec((1, RKV)),
            _const_spec((RKV, H * DN)), _const_spec((H * DV, RKV)),
            _const_spec((LANES, D)),
            _const_spec((D, RQ + H * DV)), _const_spec((1, RQ)),
            _const_spec((H * (DN + DR), RQ)),
        ],
        out_specs=(
            pl.BlockSpec((1, H, 1, DQP, T), lambda b, t: (b, 0, t, 0, 0)),
            pl.BlockSpec((1, T, H * DV), lambda b, t: (b, t, 0)),
            pl.BlockSpec((1, H, T, DN), lambda b, t: (b, 0, t, 0)),
            pl.BlockSpec((1, T, LANES), lambda b, t: (b, t, 0)),
            pl.BlockSpec((1, H, 1, DV, T), lambda b, t: (b, 0, t, 0, 0)),
        ),
        compiler_params=pltpu.CompilerParams(
            dimension_semantics=("arbitrary", "arbitrary"),
            vmem_limit_bytes=VMEM_LIMIT_BYTES),
        name="mla_proj",
    )(h, pos3, freqs, w_down_c, kv_norm_g, w_uk, w_uvT, w_drT, w_in, q_norm_g, w_uqT)


def _mla_attn_body(q_ref, kn_ref, kr_ref, v_ref, o_ref, *, NQ, TQ, DV):
    TK = TQ

    def q_block(qi, carry_unused):
        qT = q_ref[0, 0, qi]

        def step(j, carry, masked):
            m, l, acc = carry
            k0 = pl.multiple_of(j * TK, TK)
            k = jnp.concatenate([kn_ref[0, 0, pl.ds(k0, TK), :], kr_ref[0, pl.ds(k0, TK), :]],
                                axis=1)
            sT = jnp.dot(k, qT, preferred_element_type=F32)
            if masked:
                kpos = lax.broadcasted_iota(jnp.int32, (TK, TQ), 0)
                qpos = lax.broadcasted_iota(jnp.int32, (TK, TQ), 1)
                sT = jnp.where(kpos <= qpos, sT, MASK_VALUE)
            m_new = jnp.maximum(m, jnp.max(sT, axis=0, keepdims=True))
            a = jnp.exp2(m - m_new)
            p = jnp.exp2(sT - m_new)
            l = a * l + jnp.sum(p, axis=0, keepdims=True)
            pv = jnp.dot(v_ref[0, 0, j], p.astype(BF16), preferred_element_type=F32)
            return m_new, l, a * acc + pv

        init = (jnp.full((1, TQ), MASK_VALUE, F32), jnp.zeros((1, TQ), F32),
                jnp.zeros((DV, TQ), F32))
        carry = lax.fori_loop(0, qi, functools.partial(step, masked=False), init)
        _, l, acc = step(qi, carry, True)
        q0 = pl.multiple_of(qi * TQ, TQ)
        o_ref[0, 0, pl.ds(q0, TQ), :] = (acc / l).T.astype(o_ref.dtype)
        return carry_unused

    lax.fori_loop(0, NQ, q_block, 0)


def _mla_attn(qT, kn, kr, vT):
    B, H, NQ, DQP, TQ = qT.shape
    S, DN = kn.shape[2], kn.shape[3]
    DV = vT.shape[3]
    body = functools.partial(_mla_attn_body, NQ=NQ, TQ=TQ, DV=DV)
    return pl.pallas_call(
        body,
        out_shape=jax.ShapeDtypeStruct((B, H, S, DV), BF16),
        grid=(B, H),
        in_specs=[
            pl.BlockSpec((1, 1, NQ, DQP, TQ), lambda b, h: (b, h, 0, 0, 0)),
            pl.BlockSpec((1, 1, S, DN), lambda b, h: (b, h, 0, 0)),
            pl.BlockSpec((1, S, LANES), lambda b, h: (b, 0, 0)),
            pl.BlockSpec((1, 1, NQ, DV, TQ), lambda b, h: (b, h, 0, 0, 0)),
        ],
        out_specs=pl.BlockSpec((1, 1, S, DV), lambda b, h: (b, h, 0, 0)),
        compiler_params=pltpu.CompilerParams(
            dimension_semantics=("arbitrary", "arbitrary"),
            vmem_limit_bytes=VMEM_LIMIT_BYTES),
        name="mla_attn",
    )(qT, kn, kr, vT)


def _mla_out_body(o_ref, sz_ref, h_ref, w_out_ref, lg_ref, lb_ref, out_ref, *, H, alpha):
    o = jnp.concatenate([o_ref[0, hd] for hd in range(H)], axis=1).astype(F32)
    a = (o * sz_ref[0].astype(F32)).astype(BF16)
    y = jnp.dot(a, w_out_ref[...], preferred_element_type=F32)
    out_ref[0] = _layer_norm_rows(alpha * h_ref[0] + y, lg_ref[...], lb_ref[...])


def _mla_out(o, sz, h, w_out, ln_g, ln_b, alpha):
    B, H, S, DV = o.shape
    D = h.shape[-1]
    T = T_O
    body = functools.partial(_mla_out_body, H=H, alpha=alpha)
    row = lambda v: v.reshape(1, -1).astype(F32)
    return pl.pallas_call(
        body,
        out_shape=jax.ShapeDtypeStruct((B, S, D), F32),
        grid=(B, S // T),
        in_specs=[
            pl.BlockSpec((1, H, T, DV), lambda b, t: (b, 0, t, 0)),
            pl.BlockSpec((1, T, H * DV), lambda b, t: (b, t, 0)),
            pl.BlockSpec((1, T, D), lambda b, t: (b, t, 0)),
            _const_spec((H * DV, D)), _const_spec((1, D)), _const_spec((1, D)),
        ],
        out_specs=pl.BlockSpec((1, T, D), lambda b, t: (b, t, 0)),
        compiler_params=pltpu.CompilerParams(
            dimension_semantics=("arbitrary", "arbitrary"),
            vmem_limit_bytes=VMEM_LIMIT_BYTES),
        name="mla_out",
    )(o, sz, h, w_out.astype(BF16), row(ln_g), row(ln_b))


def kernel(x, positions, ln_g, ln_b, a_w_in, a_b_in, a_conv_w, a_conv_b, a_norm_g, a_norm_b,
           a_w_out, a_b_out, kv_w_down, kv_norm_g, kv_w_uk, kv_w_uv, b_w_in, b_q_norm_g, b_w_uq,
           b_w_out):
    B, S, D = x.shape
    depth = ln_g.shape[0]
    n_a, n_b = a_w_in.shape[0], b_w_in.shape[0]
    assert n_a + n_b == depth and n_b == 1, "one shared-KV attention layer is supported"
    alpha = (2.0 * depth) ** 0.25
    RKV, H, DN = kv_w_uk.shape
    DV = kv_w_uv.shape[2]
    DR = kv_w_down.shape[1] - RKV
    RQ = b_q_norm_g.shape[1]
    assert S % T_A == 0 and S % T_P == 0 and S % T_O == 0 and DR <= LANES and DN + DR <= 2 * LANES

    h = x
    for i in range(n_a):
        h = _layer_a(h, a_w_in[i], a_b_in[i], a_conv_w[i], a_conv_b[i], a_norm_g[i], a_norm_b[i],
                     a_w_out[i], a_b_out[i], ln_g[i], ln_b[i], alpha)

    freqs = (ROPE_THETA ** (-jnp.arange(0, DR, 2, dtype=F32) / DR)).reshape(DR // 2, 1)
    pos3 = positions.reshape(B, 1, S)
    w_down_c = kv_w_down[:, :RKV].astype(BF16)
    w_drT = jnp.zeros((LANES, D), BF16).at[:DR].set(kv_w_down[:, RKV:].T.astype(BF16))
    w_uk = kv_w_uk.reshape(RKV, H * DN).astype(BF16)
    w_uvT = kv_w_uv.reshape(RKV, H * DV).T.astype(BF16)
    q_scale = LOG2E / math.sqrt(DN + DR)

    j = 0
    layer = n_a + j
    w_uqT = b_w_uq[j].reshape(RQ, H * (DN + DR)).T.astype(BF16)
    qT, sz, kn, kr, vT = _mla_proj(
        h, pos3, freqs, w_down_c, kv_norm_g.reshape(1, RKV), w_uk, w_uvT, w_drT,
        b_w_in[j].astype(BF16), b_q_norm_g[j].reshape(1, RQ), w_uqT,
        H=H, DN=DN, DR=DR, DV=DV, q_scale=q_scale)
    o = _mla_attn(qT, kn, kr, vT)
    h = _mla_out(o, sz, h, b_w_out[j], ln_g[layer], ln_b[layer], alpha)
    return h
```

```python
import functools
import math

import jax
import jax.numpy as jnp
import numpy as np
from jax import lax
from jax.experimental import pallas as pl
from jax.experimental.pallas import tpu as pltpu

F32 = jnp.float32
BF16 = jnp.bfloat16

LN_EPS = 1e-5
RMS_EPS = 1e-6
ROPE_THETA = 10000.0
MASK_VALUE = -1e30
LOG2E = math.log2(math.e)

SUBLANES = 8
BF16_SUBLANES = 16
LANES = 128
VMEM_LIMIT_BYTES = 56 * 1024 * 1024

T_A = 256
CW_A = 512
T_P = 256
T_ATT = 512
ATT_UNROLL = 4
T_O = 256

_NT = (((1,), (1,)), ((), ()))


def _sigmoid(x):
    return jax.nn.sigmoid(x)


def _layer_norm_rows(h, g, b):
    mu = jnp.mean(h, axis=-1, keepdims=True)
    d = h - mu
    var = jnp.mean(d * d, axis=-1, keepdims=True)
    return d * lax.rsqrt(var + LN_EPS) * g + b


def _rms_norm_rows(x, g):
    ms = jnp.mean(x * x, axis=-1, keepdims=True)
    return x * lax.rsqrt(ms + RMS_EPS) * g


def _const_spec(shape):
    n = len(shape)
    return pl.BlockSpec(shape, lambda *_: (0,) * n, pipeline_mode=pl.Buffered(1))


def _layer_a_body(x_ref, w_in_ref, b_in_ref, cw_ref, cb_ref, ng_ref, nb_ref, w_out_ref,
                  b_out_ref, lg_ref, lb_ref, o_ref, ubuf, cbuf, zbuf, *, T, E, CW, KW, HALO,
                  alpha):
    @pl.when(pl.program_id(1) == 0)
    def _():
        ubuf[0:HALO, :] = jnp.zeros((HALO, E), F32)

    x = x_ref[0]
    xb = x.astype(BF16)
    for c in range(E // CW):
        lo = c * CW

        def proj(off):
            w = w_in_ref[:, off + lo:off + lo + CW]
            return (jnp.dot(xb, w, preferred_element_type=F32)
                    + b_in_ref[:, off + lo:off + lo + CW])

        val, gate, z = proj(0), proj(E), proj(2 * E)
        ubuf[HALO:HALO + T, lo:lo + CW] = val * _sigmoid(gate)
        zbuf[:, lo:lo + CW] = z * _sigmoid(z)
        acc = jnp.zeros((T, CW), F32)
        for k in range(KW):
            o = HALO - (KW - 1) + k
            acc = acc + ubuf[o:o + T, lo:lo + CW] * cw_ref[k:k + 1, lo:lo + CW]
        cbuf[:, lo:lo + CW] = acc + cb_ref[:, lo:lo + CW]
    ubuf[0:HALO, :] = ubuf[T:T + HALO, :]

    g = _layer_norm_rows(cbuf[...], ng_ref[...], nb_ref[...])
    a = (g * _sigmoid(g) * zbuf[...]).astype(BF16)
    y = jnp.dot(a, w_out_ref[...], preferred_element_type=F32) + b_out_ref[...]
    o_ref[0] = _layer_norm_rows(alpha * x + y, lg_ref[...], lb_ref[...])


def _layer_a(h, w_in, b_in, conv_w, conv_b, norm_g, norm_b, w_out, b_out, ln_g, ln_b, alpha):
    B, S, D = h.shape
    E = w_out.shape[0]
    KW = conv_w.shape[0]
    HALO = -(-(KW - 1) // SUBLANES) * SUBLANES
    T, CW = T_A, CW_A
    body = functools.partial(_layer_a_body, T=T, E=E, CW=CW, KW=KW, HALO=HALO, alpha=alpha)
    row = lambda v: v.reshape(1, -1).astype(F32)
    return pl.pallas_call(
        body,
        out_shape=jax.ShapeDtypeStruct((B, S, D), F32),
        grid=(B, S // T),
        in_specs=[
            pl.BlockSpec((1, T, D), lambda b, t: (b, t, 0)),
            _const_spec((D, 3 * E)), _const_spec((1, 3 * E)),
            _const_spec((KW, E)), _const_spec((1, E)),
            _const_spec((1, E)), _const_spec((1, E)),
            _const_spec((E, D)), _const_spec((1, D)),
            _const_spec((1, D)), _const_spec((1, D)),
        ],
        out_specs=pl.BlockSpec((1, T, D), lambda b, t: (b, t, 0)),
        scratch_shapes=[
            pltpu.VMEM((HALO + T, E), F32),
            pltpu.VMEM((T, E), F32),
            pltpu.VMEM((T, E), F32),
        ],
        compiler_params=pltpu.CompilerParams(
            dimension_semantics=("arbitrary", "arbitrary"),
            vmem_limit_bytes=VMEM_LIMIT_BYTES),
        name="layer_a",
    )(h, w_in.astype(BF16), row(b_in), conv_w.astype(F32), row(conv_b), row(norm_g),
      row(norm_b), w_out.astype(BF16), row(b_out), row(ln_g), row(ln_b))


def _mla_proj_body(h_ref, pos_ref, fr_ref, w_down_ref, kvg_ref, w_uk_ref, w_uvT_ref, w_drT_ref,
                   w_in_ref, qg_ref, w_uqT_ref, qT_ref, sz_ref, kn_ref, kr_ref, vT_ref, *,
                   H, RQ, DN, DR, DV, DVE, DQP, q_scale):
    hb = h_ref[0].astype(BF16)
    T = hb.shape[0]
    ang = fr_ref[...] * pos_ref[0].astype(F32)
    cosT, sinT = jnp.cos(ang), jnp.sin(ang)

    def rope_t(xT):
        x1, x2 = xT[:DR // 2], xT[DR // 2:]
        return x1 * cosT - x2 * sinT, x2 * cosT + x1 * sinT

    ckv = jnp.dot(hb, w_down_ref[...], preferred_element_type=F32)
    cn = _rms_norm_rows(ckv, kvg_ref[...]).astype(BF16)
    kn = jnp.dot(cn, w_uk_ref[...], preferred_element_type=F32)
    for hd in range(H):
        kn_ref[0, hd] = kn[:, hd * DN:(hd + 1) * DN].astype(BF16)
    vT = lax.dot_general(w_uvT_ref[...], cn, _NT, preferred_element_type=F32)
    ones_row = (lax.broadcasted_iota(jnp.int32, (DVE - DV, T), 0) == 0).astype(BF16)
    for hd in range(H):
        vT_ref[0, hd, 0, 0:DV, :] = vT[hd * DV:(hd + 1) * DV].astype(BF16)
        vT_ref[0, hd, 0, DV:DVE, :] = ones_row
    krT = lax.dot_general(w_drT_ref[...], hb, _NT, preferred_element_type=F32)
    y1, y2 = rope_t(krT[:DR])
    krT = jnp.concatenate([y1, y2, jnp.zeros((LANES - DR, T), F32)], axis=0)
    kr_ref[0] = krT.T.astype(BF16)

    proj = jnp.dot(hb, w_in_ref[...], preferred_element_type=F32)
    z = proj[:, RQ:]
    sz_ref[0] = (z * _sigmoid(z)).astype(BF16)
    cqn = _rms_norm_rows(proj[:, :RQ], qg_ref[...]).astype(BF16)
    qT = lax.dot_general(w_uqT_ref[...], cqn, _NT, preferred_element_type=F32) * q_scale
    DQ = DN + DR
    for hd in range(H):
        base = hd * DQ
        qT_ref[0, hd, 0, 0:DN, :] = qT[base:base + DN].astype(BF16)
        y1, y2 = rope_t(qT[base + DN:base + DQ])
        qT_ref[0, hd, 0, DN:DN + DR // 2, :] = y1.astype(BF16)
        qT_ref[0, hd, 0, DN + DR // 2:DQ, :] = y2.astype(BF16)
        qT_ref[0, hd, 0, DQ:DQP, :] = jnp.zeros((DQP - DQ, T), BF16)


def _mla_proj(h, pos3, freqs, w_down_c, kv_norm_g, w_uk, w_uvT, w_drT, w_in, q_norm_g, w_uqT,
              *, H, DN, DR, DV, q_scale):
    B, S, D = h.shape
    T = T_P
    NT = S // T
    RKV = w_down_c.shape[1]
    RQ = q_norm_g.shape[-1]
    DQP = 2 * LANES
    DVE = DV + BF16_SUBLANES
    body = functools.partial(_mla_proj_body, H=H, RQ=RQ, DN=DN, DR=DR, DV=DV, DVE=DVE, DQP=DQP,
                             q_scale=q_scale)
    return pl.pallas_call(
        body,
        out_shape=(
            jax.ShapeDtypeStruct((B, H, NT, DQP, T), BF16),
            jax.ShapeDtypeStruct((B, S, H * DV), BF16),
            jax.ShapeDtypeStruct((B, H, S, DN), BF16),
            jax.ShapeDtypeStruct((B, S, LANES), BF16),
            jax.ShapeDtypeStruct((B, H, NT, DVE, T), BF16),
        ),
        grid=(B, NT),
        in_specs=[
            pl.BlockSpec((1, T, D), lambda b, t: (b, t, 0)),
            pl.BlockSpec((1, 1, T), lambda b, t: (b, 0, t)),
            _const_spec((DR // 2, 1)),
            _const_spec((D, RKV)), _const_spec((1, RKV)),
            _const_spec((RKV, H * DN)), _const_spec((H * DV, RKV)),
            _const_spec((LANES, D)),
            _const_spec((D, RQ + H * DV)), _const_spec((1, RQ)),
            _const_spec((H * (DN + DR), RQ)),
        ],
        out_specs=(
            pl.BlockSpec((1, H, 1, DQP, T), lambda b, t: (b, 0, t, 0, 0)),
            pl.BlockSpec((1, T, H * DV), lambda b, t: (b, t, 0)),
            pl.BlockSpec((1, H, T, DN), lambda b, t: (b, 0, t, 0)),
            pl.BlockSpec((1, T, LANES), lambda b, t: (b, t, 0)),
            pl.BlockSpec((1, H, 1, DVE, T), lambda b, t: (b, 0, t, 0, 0)),
        ),
        compiler_params=pltpu.CompilerParams(
            dimension_semantics=("arbitrary", "arbitrary"),
            vmem_limit_bytes=VMEM_LIMIT_BYTES),
        name="mla_proj",
    )(h, pos3, freqs, w_down_c, kv_norm_g, w_uk, w_uvT, w_drT, w_in, q_norm_g, w_uqT)


def _attn_step_table(nq):
    U = ATT_UNROLL
    rows = [(0, 0, 0, 1, 0)] * 2
    rows += [(qi, j, int(j == qi), int(j == 0), qi % U) for qi in range(nq) for j in range(qi + 1)]
    n_iter = -(-len(rows) // U) * U
    rows += [(0, 0, 0, 1, nq % U)] * (n_iter + 2 - len(rows))
    for b in range(n_iter // U):
        done = set()
        for qi, j, last, first, slot in rows[b * U:(b + 1) * U]:
            assert slot not in done, "accumulator reused before its q block was stored"
            if last:
                done.add(slot)
    return np.asarray(rows, np.int32).T.copy(), n_iter


def _mla_attn_body(tab_ref, q_ref, kn_ref, kr_ref, v_ref, o_ref, s0, s1, p0, p1, bias, acc_sc, *,
                   TQ, TB, DV, N_ITER):
    R = TQ // TB
    sbuf, pbuf = (s0, s1), (p0, p1)

    @pl.when((pl.program_id(0) == 0) & (pl.program_id(1) == 0))
    def _():
        for ref in sbuf + pbuf + (acc_sc,):
            ref[...] = jnp.zeros(ref.shape, ref.dtype)
        kpos = lax.broadcasted_iota(jnp.int32, (TQ, TQ), 0)
        qpos = lax.broadcasted_iota(jnp.int32, (TQ, TQ), 1)
        bias[0] = jnp.full((TQ, TQ), jnp.inf, F32)
        bias[1] = jnp.where(kpos <= qpos, jnp.inf, MASK_VALUE).astype(F32)

    def scores(c, slot):
        qi, j, diag = tab_ref[0, c], tab_ref[1, c], tab_ref[2, c]
        k0 = pl.multiple_of(j * TQ, TQ)
        k = jnp.concatenate([kn_ref[0, 0, pl.ds(k0, TQ), :], kr_ref[0, pl.ds(k0, TQ), :]], axis=1)
        qT = jnp.concatenate([q_ref[0, 0, qi * R + r] for r in range(R)], axis=1)
        sT = jnp.minimum(jnp.dot(k, qT, preferred_element_type=F32), bias[diag])
        sbuf[slot][...] = sT
        return jnp.max(sT, axis=0, keepdims=True)

    def softmax(c, slot, m, cmax):
        m = jnp.where(tab_ref[3, c] == 1, MASK_VALUE, m)
        m_new = jnp.maximum(m, cmax)
        pbuf[slot][...] = jnp.exp2(sbuf[slot][...] - m_new).astype(BF16)
        return m_new, jnp.exp2(m - m_new)

    def values(c, slot, alpha):
        j, a = tab_ref[1, c], tab_ref[4, c]
        vT = jnp.concatenate([v_ref[0, 0, j * R + r] for r in range(R)], axis=1)
        old = jnp.where(tab_ref[3, c] == 1, 0.0, alpha * acc_sc[a])
        acc_sc[a] = old + jnp.dot(vT, pbuf[slot][...], preferred_element_type=F32)

    def finish(c):
        @pl.when(tab_ref[2, c] == 1)
        def _():
            q0 = pl.multiple_of(tab_ref[0, c] * TQ, TQ)
            acc = acc_sc[tab_ref[4, c]]
            o = acc[:DV] * (1.0 / acc[DV:DV + 1])
            o_ref[0, 0, pl.ds(q0, TQ), :] = o.T.astype(o_ref.dtype)

    def unrolled_iterations(iu, carry):
        m, alpha, cmax = carry
        for u in range(ATT_UNROLL):
            i = ATT_UNROLL * iu + u
            values(i, u % 2, alpha)
            m, alpha = softmax(i + 1, (u + 1) % 2, m, cmax)
            cmax = scores(i + 2, u % 2)
        for u in range(ATT_UNROLL):
            finish(ATT_UNROLL * iu + u)
        return m, alpha, cmax

    init = (jnp.full((1, TQ), MASK_VALUE, F32), jnp.zeros((1, TQ), F32), jnp.zeros((1, TQ), F32))
    lax.fori_loop(0, N_ITER // ATT_UNROLL, unrolled_iterations, init)


def _mla_attn(qT, kn, kr, vT):
    B, H, NB, DQP, TB = qT.shape
    S, DN = kn.shape[2], kn.shape[3]
    DVE = vT.shape[3]
    DV = DVE - BF16_SUBLANES
    TQ = T_ATT
    table, n_iter = _attn_step_table(S // TQ)
    body = functools.partial(_mla_attn_body, TQ=TQ, TB=TB, DV=DV, N_ITER=n_iter)
    grid_spec = pltpu.PrefetchScalarGridSpec(
        num_scalar_prefetch=1,
        grid=(B, H),
        in_specs=[
            pl.BlockSpec((1, 1, NB, DQP, TB), lambda b, h, tab: (b, h, 0, 0, 0)),
            pl.BlockSpec((1, 1, S, DN), lambda b, h, tab: (b, h, 0, 0)),
            pl.BlockSpec((1, S, LANES), lambda b, h, tab: (b, 0, 0)),
            pl.BlockSpec((1, 1, NB, DVE, TB), lambda b, h, tab: (b, h, 0, 0, 0)),
        ],
        out_specs=pl.BlockSpec((1, 1, S, DV), lambda b, h, tab: (b, h, 0, 0)),
        scratch_shapes=[
            pltpu.VMEM((TQ, TQ), F32), pltpu.VMEM((TQ, TQ), F32),
            pltpu.VMEM((TQ, TQ), BF16), pltpu.VMEM((TQ, TQ), BF16),
            pltpu.VMEM((2, TQ, TQ), F32),
            pltpu.VMEM((ATT_UNROLL, DVE, TQ), F32),
        ])
    return pl.pallas_call(
        body,
        out_shape=jax.ShapeDtypeStruct((B, H, S, DV), BF16),
        grid_spec=grid_spec,
        compiler_params=pltpu.CompilerParams(
            dimension_semantics=("arbitrary", "arbitrary"),
            vmem_limit_bytes=VMEM_LIMIT_BYTES),
        name="mla_attn",
    )(jnp.asarray(table), qT, kn, kr, vT)


def _mla_out_body(o_ref, sz_ref, h_ref, w_out_ref, lg_ref, lb_ref, out_ref, *, H, alpha):
    o = jnp.concatenate([o_ref[0, hd] for hd in range(H)], axis=1).astype(F32)
    a = (o * sz_ref[0].astype(F32)).astype(BF16)
    y = jnp.dot(a, w_out_ref[...], preferred_element_type=F32)
    out_ref[0] = _layer_norm_rows(alpha * h_ref[0] + y, lg_ref[...], lb_ref[...])


def _mla_out(o, sz, h, w_out, ln_g, ln_b, alpha):
    B, H, S, DV = o.shape
    D = h.shape[-1]
    T = T_O
    body = functools.partial(_mla_out_body, H=H, alpha=alpha)
    row = lambda v: v.reshape(1, -1).astype(F32)
    return pl.pallas_call(
        body,
        out_shape=jax.ShapeDtypeStruct((B, S, D), F32),
        grid=(B, S // T),
        in_specs=[
            pl.BlockSpec((1, H, T, DV), lambda b, t: (b, 0, t, 0)),
            pl.BlockSpec((1, T, H * DV), lambda b, t: (b, t, 0)),
            pl.BlockSpec((1, T, D), lambda b, t: (b, t, 0)),
            _const_spec((H * DV, D)), _const_spec((1, D)), _const_spec((1, D)),
        ],
        out_specs=pl.BlockSpec((1, T, D), lambda b, t: (b, t, 0)),
        compiler_params=pltpu.CompilerParams(
            dimension_semantics=("arbitrary", "arbitrary"),
            vmem_limit_bytes=VMEM_LIMIT_BYTES),
        name="mla_out",
    )(o, sz, h, w_out.astype(BF16), row(ln_g), row(ln_b))


def kernel(x, positions, ln_g, ln_b, a_w_in, a_b_in, a_conv_w, a_conv_b, a_norm_g, a_norm_b,
           a_w_out, a_b_out, kv_w_down, kv_norm_g, kv_w_uk, kv_w_uv, b_w_in, b_q_norm_g, b_w_uq,
           b_w_out):
    B, S, D = x.shape
    depth = ln_g.shape[0]
    n_a, n_b = a_w_in.shape[0], b_w_in.shape[0]
    assert n_a + n_b == depth and n_b == 1, "one shared-KV attention layer is supported"
    alpha = (2.0 * depth) ** 0.25
    RKV, H, DN = kv_w_uk.shape
    DV = kv_w_uv.shape[2]
    DR = kv_w_down.shape[1] - RKV
    RQ = b_q_norm_g.shape[1]
    assert S % T_A == 0 and S % T_ATT == 0 and T_ATT % T_P == 0 and S % T_O == 0
    assert DR <= LANES and DN + DR <= 2 * LANES

    h = x
    for i in range(n_a):
        h = _layer_a(h, a_w_in[i], a_b_in[i], a_conv_w[i], a_conv_b[i], a_norm_g[i], a_norm_b[i],
                     a_w_out[i], a_b_out[i], ln_g[i], ln_b[i], alpha)

    freqs = (ROPE_THETA ** (-jnp.arange(0, DR, 2, dtype=F32) / DR)).reshape(DR // 2, 1)
    pos3 = positions.reshape(B, 1, S)
    w_down_c = kv_w_down[:, :RKV].astype(BF16)
    w_drT = jnp.zeros((LANES, D), BF16).at[:DR].set(kv_w_down[:, RKV:].T.astype(BF16))
    w_uk = kv_w_uk.reshape(RKV, H * DN).astype(BF16)
    w_uvT = kv_w_uv.reshape(RKV, H * DV).T.astype(BF16)
    q_scale = LOG2E / math.sqrt(DN + DR)

    j = 0
    layer = n_a + j
    w_uqT = b_w_uq[j].reshape(RQ, H * (DN + DR)).T.astype(BF16)
    qT, sz, kn, kr, vT = _mla_proj(
        h, pos3, freqs, w_down_c, kv_norm_g.reshape(1, RKV), w_uk, w_uvT, w_drT,
        b_w_in[j].astype(BF16), b_q_norm_g[j].reshape(1, RQ), w_uqT,
        H=H, DN=DN, DR=DR, DV=DV, q_scale=q_scale)
    o = _mla_attn(qT, kn, kr, vT)
    h = _mla_out(o, sz, h, b_w_out[j], ln_g[layer], ln_b[layer], alpha)
    return h
```

```python
import functools
import math

import jax
import jax.numpy as jnp
import numpy as np
from jax import lax
from jax.experimental import pallas as pl
from jax.experimental.pallas import tpu as pltpu

F32 = jnp.float32
BF16 = jnp.bfloat16

LN_EPS = 1e-5
RMS_EPS = 1e-6
ROPE_THETA = 10000.0
MASK_VALUE = -1e30
LOG2E = math.log2(math.e)

SUBLANES = 8
BF16_SUBLANES = 16
LANES = 128
VMEM_LIMIT_BYTES = 56 * 1024 * 1024

T_A = 256
CW_A = 512
T_P = 256
T_ATT = 512
ATT_UNROLL = 4
ATT_HEADS = 4
T_O = 256

_NT = (((1,), (1,)), ((), ()))


def _sigmoid(x):
    return jax.nn.sigmoid(x)


def _layer_norm_rows(h, g, b):
    mu = jnp.mean(h, axis=-1, keepdims=True)
    d = h - mu
    var = jnp.mean(d * d, axis=-1, keepdims=True)
    return d * lax.rsqrt(var + LN_EPS) * g + b


def _rms_norm_rows(x, g):
    ms = jnp.mean(x * x, axis=-1, keepdims=True)
    return x * lax.rsqrt(ms + RMS_EPS) * g


def _const_spec(shape):
    n = len(shape)
    return pl.BlockSpec(shape, lambda *_: (0,) * n, pipeline_mode=pl.Buffered(1))


def _layer_a_body(x_ref, w_in_ref, b_in_ref, cw_ref, cb_ref, ng_ref, nb_ref, w_out_ref,
                  b_out_ref, lg_ref, lb_ref, o_ref, ubuf, cbuf, zbuf, *, T, E, CW, KW, HALO,
                  alpha):
    @pl.when(pl.program_id(1) == 0)
    def _():
        ubuf[0:HALO, :] = jnp.zeros((HALO, E), F32)

    x = x_ref[0]
    xb = x.astype(BF16)
    for c in range(E // CW):
        lo = c * CW

        def proj(off):
            w = w_in_ref[:, off + lo:off + lo + CW]
            return (jnp.dot(xb, w, preferred_element_type=F32)
                    + b_in_ref[:, off + lo:off + lo + CW])

        val, gate, z = proj(0), proj(E), proj(2 * E)
        ubuf[HALO:HALO + T, lo:lo + CW] = val * _sigmoid(gate)
        zbuf[:, lo:lo + CW] = z * _sigmoid(z)
        for lt in range(CW // LANES):
            cs = slice(lo + lt * LANES, lo + (lt + 1) * LANES)
            col = ubuf[:, cs]
            acc = None
            for b in range(SUBLANES):
                win = col if b == 0 else pltpu.roll(col, HALO + T - b, axis=0)
                for k in range(KW):
                    o = HALO - (KW - 1) + k
                    if o % SUBLANES != b:
                        continue
                    term = win[o - b:o - b + T] * cw_ref[k:k + 1, cs]
                    acc = term if acc is None else acc + term
            cbuf[:, cs] = acc + cb_ref[:, cs]
    ubuf[0:HALO, :] = ubuf[T:T + HALO, :]

    g = _layer_norm_rows(cbuf[...], ng_ref[...], nb_ref[...])
    a = (g * _sigmoid(g) * zbuf[...]).astype(BF16)
    y = jnp.dot(a, w_out_ref[...], preferred_element_type=F32) + b_out_ref[...]
    o_ref[0] = _layer_norm_rows(alpha * x + y, lg_ref[...], lb_ref[...])


def _layer_a(h, w_in, b_in, conv_w, conv_b, norm_g, norm_b, w_out, b_out, ln_g, ln_b, alpha):
    B, S, D = h.shape
    E = w_out.shape[0]
    KW = conv_w.shape[0]
    HALO = -(-(KW - 1) // SUBLANES) * SUBLANES
    T, CW = T_A, CW_A
    body = functools.partial(_layer_a_body, T=T, E=E, CW=CW, KW=KW, HALO=HALO, alpha=alpha)
    row = lambda v: v.reshape(1, -1).astype(F32)
    return pl.pallas_call(
        body,
        out_shape=jax.ShapeDtypeStruct((B, S, D), F32),
        grid=(B, S // T),
        in_specs=[
            pl.BlockSpec((1, T, D), lambda b, t: (b, t, 0)),
            _const_spec((D, 3 * E)), _const_spec((1, 3 * E)),
            _const_spec((KW, E)), _const_spec((1, E)),
            _const_spec((1, E)), _const_spec((1, E)),
            _const_spec((E, D)), _const_spec((1, D)),
            _const_spec((1, D)), _const_spec((1, D)),
        ],
        out_specs=pl.BlockSpec((1, T, D), lambda b, t: (b, t, 0)),
        scratch_shapes=[
            pltpu.VMEM((HALO + T, E), F32),
            pltpu.VMEM((T, E), F32),
            pltpu.VMEM((T, E), F32),
        ],
        compiler_params=pltpu.CompilerParams(
            dimension_semantics=("arbitrary", "arbitrary"),
            vmem_limit_bytes=VMEM_LIMIT_BYTES),
        name="layer_a",
    )(h, w_in.astype(BF16), row(b_in), conv_w.astype(F32), row(conv_b), row(norm_g),
      row(norm_b), w_out.astype(BF16), row(b_out), row(ln_g), row(ln_b))


def _mla_proj_body(h_ref, pos_ref, fr_ref, w_down_ref, kvg_ref, w_uk_ref, w_uvT_ref, w_drT_ref,
                   w_in_ref, qg_ref, w_uqT_ref, qT_ref, sz_ref, kn_ref, kr_ref, vT_ref, *,
                   H, RQ, DN, DR, DV, DVE, DQP, q_scale):
    hb = h_ref[0].astype(BF16)
    T = hb.shape[0]
    ang = fr_ref[...] * pos_ref[0].astype(F32)
    cosT, sinT = jnp.cos(ang), jnp.sin(ang)

    def rope_t(xT):
        x1, x2 = xT[:DR // 2], xT[DR // 2:]
        return x1 * cosT - x2 * sinT, x2 * cosT + x1 * sinT

    ckv = jnp.dot(hb, w_down_ref[...], preferred_element_type=F32)
    cn = _rms_norm_rows(ckv, kvg_ref[...]).astype(BF16)
    kn = jnp.dot(cn, w_uk_ref[...], preferred_element_type=F32)
    for hd in range(H):
        kn_ref[0, hd] = kn[:, hd * DN:(hd + 1) * DN].astype(BF16)
    vT = lax.dot_general(w_uvT_ref[...], cn, _NT, preferred_element_type=F32)
    ones_row = (lax.broadcasted_iota(jnp.int32, (DVE - DV, T), 0) == 0).astype(BF16)
    for hd in range(H):
        vT_ref[0, hd, 0, 0:DV, :] = vT[hd * DV:(hd + 1) * DV].astype(BF16)
        vT_ref[0, hd, 0, DV:DVE, :] = ones_row
    krT = lax.dot_general(w_drT_ref[...], hb, _NT, preferred_element_type=F32)
    y1, y2 = rope_t(krT[:DR])
    krT = jnp.concatenate([y1, y2, jnp.zeros((LANES - DR, T), F32)], axis=0)
    kr_ref[0] = krT.T.astype(BF16)

    proj = jnp.dot(hb, w_in_ref[...], preferred_element_type=F32)
    z = proj[:, RQ:]
    sz_ref[0] = (z * _sigmoid(z)).astype(BF16)
    cqn = _rms_norm_rows(proj[:, :RQ], qg_ref[...]).astype(BF16)
    qT = lax.dot_general(w_uqT_ref[...], cqn, _NT, preferred_element_type=F32) * q_scale
    DQ = DN + DR
    for hd in range(H):
        base = hd * DQ
        qT_ref[0, hd, 0, 0:DN, :] = qT[base:base + DN].astype(BF16)
        y1, y2 = rope_t(qT[base + DN:base + DQ])
        qT_ref[0, hd, 0, DN:DN + DR // 2, :] = y1.astype(BF16)
        qT_ref[0, hd, 0, DN + DR // 2:DQ, :] = y2.astype(BF16)
        qT_ref[0, hd, 0, DQ:DQP, :] = jnp.zeros((DQP - DQ, T), BF16)


def _mla_proj(h, pos3, freqs, w_down_c, kv_norm_g, w_uk, w_uvT, w_drT, w_in, q_norm_g, w_uqT,
              *, H, DN, DR, DV, q_scale):
    B, S, D = h.shape
    T = T_P
    NT = S // T
    RKV = w_down_c.shape[1]
    RQ = q_norm_g.shape[-1]
    DQP = 2 * LANES
    DVE = DV + BF16_SUBLANES
    body = functools.partial(_mla_proj_body, H=H, RQ=RQ, DN=DN, DR=DR, DV=DV, DVE=DVE, DQP=DQP,
                             q_scale=q_scale)
    return pl.pallas_call(
        body,
        out_shape=(
            jax.ShapeDtypeStruct((B, H, NT, DQP, T), BF16),
            jax.ShapeDtypeStruct((B, S, H * DV), BF16),
            jax.ShapeDtypeStruct((B, H, S, DN), BF16),
            jax.ShapeDtypeStruct((B, S, LANES), BF16),
            jax.ShapeDtypeStruct((B, H, NT, DVE, T), BF16),
        ),
        grid=(B, NT),
        in_specs=[
            pl.BlockSpec((1, T, D), lambda b, t: (b, t, 0)),
            pl.BlockSpec((1, 1, T), lambda b, t: (b, 0, t)),
            _const_spec((DR // 2, 1)),
            _const_spec((D, RKV)), _const_spec((1, RKV)),
            _const_spec((RKV, H * DN)), _const_spec((H * DV, RKV)),
            _const_spec((LANES, D)),
            _const_spec((D, RQ + H * DV)), _const_spec((1, RQ)),
            _const_spec((H * (DN + DR), RQ)),
        ],
        out_specs=(
            pl.BlockSpec((1, H, 1, DQP, T), lambda b, t: (b, 0, t, 0, 0)),
            pl.BlockSpec((1, T, H * DV), lambda b, t: (b, t, 0)),
            pl.BlockSpec((1, H, T, DN), lambda b, t: (b, 0, t, 0)),
            pl.BlockSpec((1, T, LANES), lambda b, t: (b, t, 0)),
            pl.BlockSpec((1, H, 1, DVE, T), lambda b, t: (b, 0, t, 0, 0)),
        ),
        compiler_params=pltpu.CompilerParams(
            dimension_semantics=("arbitrary", "arbitrary"),
            vmem_limit_bytes=VMEM_LIMIT_BYTES),
        name="mla_proj",
    )(h, pos3, freqs, w_down_c, kv_norm_g, w_uk, w_uvT, w_drT, w_in, q_norm_g, w_uqT)


STEP_FIRST, STEP_PAD = 1, 2


def _attn_step_table(n_heads, nq):
    U = ATT_UNROLL
    rows = [(0, 0, 0, STEP_PAD, 0, 0)] * 2
    rows += [(qi, j, int(j == qi), STEP_FIRST if j == 0 else 0, (hd * nq + qi) % U, hd)
             for hd in range(n_heads) for qi in range(nq) for j in range(qi + 1)]
    n_iter = -(-len(rows) // U) * U
    rows += [(0, 0, 0, STEP_PAD, (n_heads * nq) % U, 0)] * (n_iter + 2 - len(rows))
    for b in range(n_iter // U):
        done = set()
        for _, _, last, _, slot, _ in rows[b * U:(b + 1) * U]:
            assert slot not in done, "accumulator reused before its q block was stored"
            if last:
                done.add(slot)
    return np.asarray(rows, np.int32).T.copy(), n_iter


def _mla_attn_body(tab_ref, q_ref, kn_ref, kr_ref, v_ref, o_ref, s0, s1, p0, p1, bias, acc_sc, *,
                   TQ, TB, DV, N_ITER):
    R = TQ // TB
    sbuf, pbuf = (s0, s1), (p0, p1)

    @pl.when((pl.program_id(0) == 0) & (pl.program_id(1) == 0))
    def _():
        for ref in sbuf + pbuf + (acc_sc,):
            ref[...] = jnp.zeros(ref.shape, ref.dtype)
        kpos = lax.broadcasted_iota(jnp.int32, (TQ, TQ), 0)
        qpos = lax.broadcasted_iota(jnp.int32, (TQ, TQ), 1)
        bias[0] = jnp.full((TQ, TQ), jnp.inf, F32)
        bias[1] = jnp.where(kpos <= qpos, jnp.inf, MASK_VALUE).astype(F32)

    def scores(c, slot):
        qi, j, diag, hd = tab_ref[0, c], tab_ref[1, c], tab_ref[2, c], tab_ref[5, c]
        k0 = pl.multiple_of(j * TQ, TQ)
        k = jnp.concatenate([kn_ref[0, hd, pl.ds(k0, TQ), :], kr_ref[0, pl.ds(k0, TQ), :]], axis=1)
        qT = jnp.concatenate([q_ref[0, hd, qi * R + r] for r in range(R)], axis=1)
        sT = jnp.minimum(jnp.dot(k, qT, preferred_element_type=F32), bias[diag])
        sbuf[slot][...] = sT
        return jnp.max(sT, axis=0, keepdims=True)

    def softmax(c, slot, m, cmax):
        start = tab_ref[3, c]
        m = jnp.where(start != 0, MASK_VALUE, m)
        m_new = jnp.maximum(m, jnp.where(start == STEP_PAD, -MASK_VALUE, cmax))
        pbuf[slot][...] = jnp.exp2(sbuf[slot][...] - m_new).astype(BF16)
        return m_new, jnp.exp2(m - m_new)

    def values(c, slot, alpha):
        j, a, hd = tab_ref[1, c], tab_ref[4, c], tab_ref[5, c]
        vT = jnp.concatenate([v_ref[0, hd, j * R + r] for r in range(R)], axis=1)
        acc_sc[a] = alpha * acc_sc[a] + jnp.dot(vT, pbuf[slot][...],
                                                preferred_element_type=F32)

    def finish(c):
        @pl.when(tab_ref[2, c] == 1)
        def _():
            q0 = pl.multiple_of(tab_ref[0, c] * TQ, TQ)
            acc = acc_sc[tab_ref[4, c]]
            o = acc[:DV] * (1.0 / acc[DV:DV + 1])
            o_ref[0, tab_ref[5, c], pl.ds(q0, TQ), :] = o.T.astype(o_ref.dtype)

    def unrolled_iterations(iu, carry):
        m, alpha, cmax = carry
        for u in range(ATT_UNROLL):
            i = ATT_UNROLL * iu + u
            cmax_next = scores(i + 2, u % 2)
            m, alpha_next = softmax(i + 1, (u + 1) % 2, m, cmax)
            values(i, u % 2, alpha)
            cmax, alpha = cmax_next, alpha_next
        for u in range(ATT_UNROLL):
            finish(ATT_UNROLL * iu + u)
        return m, alpha, cmax

    init = (jnp.full((1, TQ), MASK_VALUE, F32), jnp.zeros((1, TQ), F32), jnp.zeros((1, TQ), F32))
    lax.fori_loop(0, N_ITER // ATT_UNROLL, unrolled_iterations, init)


def _mla_attn(qT, kn, kr, vT):
    B, H, NB, DQP, TB = qT.shape
    S, DN = kn.shape[2], kn.shape[3]
    DVE = vT.shape[3]
    DV = DVE - BF16_SUBLANES
    TQ, G = T_ATT, ATT_HEADS
    assert H % G == 0
    table, n_iter = _attn_step_table(G, S // TQ)
    body = functools.partial(_mla_attn_body, TQ=TQ, TB=TB, DV=DV, N_ITER=n_iter)
    grid_spec = pltpu.PrefetchScalarGridSpec(
        num_scalar_prefetch=1,
        grid=(B, H // G),
        in_specs=[
            pl.BlockSpec((1, G, NB, DQP, TB), lambda b, h, tab: (b, h, 0, 0, 0)),
            pl.BlockSpec((1, G, S, DN), lambda b, h, tab: (b, h, 0, 0)),
            pl.BlockSpec((1, S, LANES), lambda b, h, tab: (b, 0, 0)),
            pl.BlockSpec((1, G, NB, DVE, TB), lambda b, h, tab: (b, h, 0, 0, 0)),
        ],
        out_specs=pl.BlockSpec((1, G, S, DV), lambda b, h, tab: (b, h, 0, 0)),
        scratch_shapes=[
            pltpu.VMEM((TQ, TQ), F32), pltpu.VMEM((TQ, TQ), F32),
            pltpu.VMEM((TQ, TQ), BF16), pltpu.VMEM((TQ, TQ), BF16),
            pltpu.VMEM((2, TQ, TQ), F32),
            pltpu.VMEM((ATT_UNROLL, DVE, TQ), F32),
        ])
    return pl.pallas_call(
        body,
        out_shape=jax.ShapeDtypeStruct((B, H, S, DV), BF16),
        grid_spec=grid_spec,
        compiler_params=pltpu.CompilerParams(
            dimension_semantics=("arbitrary", "arbitrary"),
            vmem_limit_bytes=VMEM_LIMIT_BYTES),
        name="mla_attn",
    )(jnp.asarray(table), qT, kn, kr, vT)


def _mla_out_body(o_ref, sz_ref, h_ref, w_out_ref, lg_ref, lb_ref, out_ref, *, H, alpha):
    o = jnp.concatenate([o_ref[0, hd] for hd in range(H)], axis=1).astype(F32)
    a = (o * sz_ref[0].astype(F32)).astype(BF16)
    y = jnp.dot(a, w_out_ref[...], preferred_element_type=F32)
    out_ref[0] = _layer_norm_rows(alpha * h_ref[0] + y, lg_ref[...], lb_ref[...])


def _mla_out(o, sz, h, w_out, ln_g, ln_b, alpha):
    B, H, S, DV = o.shape
    D = h.shape[-1]
    T = T_O
    body = functools.partial(_mla_out_body, H=H, alpha=alpha)
    row = lambda v: v.reshape(1, -1).astype(F32)
    return pl.pallas_call(
        body,
        out_shape=jax.ShapeDtypeStruct((B, S, D), F32),
        grid=(B, S // T),
        in_specs=[
            pl.BlockSpec((1, H, T, DV), lambda b, t: (b, 0, t, 0)),
            pl.BlockSpec((1, T, H * DV), lambda b, t: (b, t, 0)),
            pl.BlockSpec((1, T, D), lambda b, t: (b, t, 0)),
            _const_spec((H * DV, D)), _const_spec((1, D)), _const_spec((1, D)),
        ],
        out_specs=pl.BlockSpec((1, T, D), lambda b, t: (b, t, 0)),
        compiler_params=pltpu.CompilerParams(
            dimension_semantics=("arbitrary", "arbitrary"),
            vmem_limit_bytes=VMEM_LIMIT_BYTES),
        name="mla_out",
    )(o, sz, h, w_out.astype(BF16), row(ln_g), row(ln_b))


def kernel(x, positions, ln_g, ln_b, a_w_in, a_b_in, a_conv_w, a_conv_b, a_norm_g, a_norm_b,
           a_w_out, a_b_out, kv_w_down, kv_norm_g, kv_w_uk, kv_w_uv, b_w_in, b_q_norm_g, b_w_uq,
           b_w_out):
    B, S, D = x.shape
    depth = ln_g.shape[0]
    n_a, n_b = a_w_in.shape[0], b_w_in.shape[0]
    assert n_a + n_b == depth and n_b == 1, "one shared-KV attention layer is supported"
    alpha = (2.0 * depth) ** 0.25
    RKV, H, DN = kv_w_uk.shape
    DV = kv_w_uv.shape[2]
    DR = kv_w_down.shape[1] - RKV
    RQ = b_q_norm_g.shape[1]
    assert S % T_A == 0 and S % T_ATT == 0 and T_ATT % T_P == 0 and S % T_O == 0
    assert DR <= LANES and DN + DR <= 2 * LANES

    h = x
    for i in range(n_a):
        h = _layer_a(h, a_w_in[i], a_b_in[i], a_conv_w[i], a_conv_b[i], a_norm_g[i], a_norm_b[i],
                     a_w_out[i], a_b_out[i], ln_g[i], ln_b[i], alpha)

    freqs = (ROPE_THETA ** (-jnp.arange(0, DR, 2, dtype=F32) / DR)).reshape(DR // 2, 1)
    pos3 = positions.reshape(B, 1, S)
    w_down_c = kv_w_down[:, :RKV].astype(BF16)
    w_drT = jnp.zeros((LANES, D), BF16).at[:DR].set(kv_w_down[:, RKV:].T.astype(BF16))
    w_uk = kv_w_uk.reshape(RKV, H * DN).astype(BF16)
    w_uvT = kv_w_uv.reshape(RKV, H * DV).T.astype(BF16)
    q_scale = LOG2E / math.sqrt(DN + DR)

    j = 0
    layer = n_a + j
    w_uqT = b_w_uq[j].reshape(RQ, H * (DN + DR)).T.astype(BF16)
    qT, sz, kn, kr, vT = _mla_proj(
        h, pos3, freqs, w_down_c, kv_norm_g.reshape(1, RKV), w_uk, w_uvT, w_drT,
        b_w_in[j].astype(BF16), b_q_norm_g[j].reshape(1, RQ), w_uqT,
        H=H, DN=DN, DR=DR, DV=DV, q_scale=q_scale)
    o = _mla_attn(qT, kn, kr, vT)
    h = _mla_out(o, sz, h, b_w_out[j], ln_g[layer], ln_b[layer], alpha)
    return h
```

```python
import functools
import math

import jax
import jax.numpy as jnp
import numpy as np
from jax import lax
from jax.experimental import pallas as pl
from jax.experimental.pallas import tpu as pltpu

F32 = jnp.float32
BF16 = jnp.bfloat16

LN_EPS = 1e-5
RMS_EPS = 1e-6
ROPE_THETA = 10000.0
MASK_VALUE = -1e30
LOG2E = math.log2(math.e)

SUBLANES = 8
BF16_SUBLANES = 16
LANES = 128
VMEM_LIMIT_BYTES = 56 * 1024 * 1024

T_A = 256
CW_A = 512
U_PITCH = 2
T_P = 256
T_ATT = 512
ATT_UNROLL = 4
ATT_HEADS = 4
ATT_STAGES = 2
T_O = 256

_NT = (((1,), (1,)), ((), ()))


def _sigmoid(x):
    return jax.nn.sigmoid(x)


def _layer_norm_rows(h, g, b):
    mu = jnp.mean(h, axis=-1, keepdims=True)
    d = h - mu
    var = jnp.mean(d * d, axis=-1, keepdims=True)
    return d * lax.rsqrt(var + LN_EPS) * g + b


def _rms_norm_rows(x, g):
    ms = jnp.mean(x * x, axis=-1, keepdims=True)
    return x * lax.rsqrt(ms + RMS_EPS) * g


def _const_spec(shape):
    n = len(shape)
    return pl.BlockSpec(shape, lambda *_: (0,) * n, pipeline_mode=pl.Buffered(1))


def _layer_a_body(x_ref, w_in_ref, b_in_ref, cw_ref, cb_ref, ng_ref, nb_ref, w_out_ref,
                  b_out_ref, lg_ref, lb_ref, o_ref, ubuf, cbuf, zbuf, *, T, E, CW, KW, HALO,
                  alpha):
    def usteps(start, size):
        return pl.ds(U_PITCH * start, size, stride=U_PITCH)

    @pl.when(pl.program_id(1) == 0)
    def _():
        ubuf[:, 0:U_PITCH * HALO, :] = jnp.zeros((E // LANES, U_PITCH * HALO, LANES), F32)

    x = x_ref[0]
    xb = x.astype(BF16)
    for c in range(E // CW):
        lo = c * CW

        def proj(off):
            w = w_in_ref[:, off + lo:off + lo + CW]
            return (jnp.dot(xb, w, preferred_element_type=F32)
                    + b_in_ref[:, off + lo:off + lo + CW])

        val, gate, z = proj(0), proj(E), proj(2 * E)
        u = val * _sigmoid(gate)
        zbuf[:, lo:lo + CW] = z * _sigmoid(z)
        for lt in range(CW // LANES):
            cs = slice(lo + lt * LANES, lo + (lt + 1) * LANES)
            s = lo // LANES + lt
            ubuf[s, usteps(HALO, T), :] = u[:, lt * LANES:(lt + 1) * LANES]
            acc = None
            for k in range(KW):
                term = ubuf[s, usteps(HALO - (KW - 1) + k, T), :] * cw_ref[k:k + 1, cs]
                acc = term if acc is None else acc + term
            cbuf[:, cs] = acc + cb_ref[:, cs]
            ubuf[s, usteps(0, HALO), :] = ubuf[s, usteps(T, HALO), :]

    g = _layer_norm_rows(cbuf[...], ng_ref[...], nb_ref[...])
    a = (g * _sigmoid(g) * zbuf[...]).astype(BF16)
    y = jnp.dot(a, w_out_ref[...], preferred_element_type=F32) + b_out_ref[...]
    o_ref[0] = _layer_norm_rows(alpha * x + y, lg_ref[...], lb_ref[...])


def _layer_a(h, w_in, b_in, conv_w, conv_b, norm_g, norm_b, w_out, b_out, ln_g, ln_b, alpha):
    B, S, D = h.shape
    E = w_out.shape[0]
    KW = conv_w.shape[0]
    HALO = -(-(KW - 1) // SUBLANES) * SUBLANES
    T, CW = T_A, CW_A
    body = functools.partial(_layer_a_body, T=T, E=E, CW=CW, KW=KW, HALO=HALO, alpha=alpha)
    row = lambda v: v.reshape(1, -1).astype(F32)
    return pl.pallas_call(
        body,
        out_shape=jax.ShapeDtypeStruct((B, S, D), F32),
        grid=(B, S // T),
        in_specs=[
            pl.BlockSpec((1, T, D), lambda b, t: (b, t, 0)),
            _const_spec((D, 3 * E)), _const_spec((1, 3 * E)),
            _const_spec((KW, E)), _const_spec((1, E)),
            _const_spec((1, E)), _const_spec((1, E)),
            _const_spec((E, D)), _const_spec((1, D)),
            _const_spec((1, D)), _const_spec((1, D)),
        ],
        out_specs=pl.BlockSpec((1, T, D), lambda b, t: (b, t, 0)),
        scratch_shapes=[
            pltpu.VMEM((E // LANES, U_PITCH * (HALO + T), LANES), F32),
            pltpu.VMEM((T, E), F32),
            pltpu.VMEM((T, E), F32),
        ],
        compiler_params=pltpu.CompilerParams(
            dimension_semantics=("arbitrary", "arbitrary"),
            vmem_limit_bytes=VMEM_LIMIT_BYTES),
        name="layer_a",
    )(h, w_in.astype(BF16), row(b_in), conv_w.astype(F32), row(conv_b), row(norm_g),
      row(norm_b), w_out.astype(BF16), row(b_out), row(ln_g), row(ln_b))


def _mla_proj_body(h_ref, pos_ref, fr_ref, w_down_ref, kvg_ref, w_uk_ref, w_uvT_ref, w_drT_ref,
                   w_in_ref, qg_ref, w_uqT_ref, qT_ref, sz_ref, kn_ref, kr_ref, vT_ref, *,
                   H, RQ, DN, DR, DV, DVE, DQP, q_scale):
    hb = h_ref[0].astype(BF16)
    T = hb.shape[0]
    ang = fr_ref[...] * pos_ref[0].astype(F32)
    cosT, sinT = jnp.cos(ang), jnp.sin(ang)

    def rope_t(xT):
        x1, x2 = xT[:DR // 2], xT[DR // 2:]
        return x1 * cosT - x2 * sinT, x2 * cosT + x1 * sinT

    ckv = jnp.dot(hb, w_down_ref[...], preferred_element_type=F32)
    cn = _rms_norm_rows(ckv, kvg_ref[...]).astype(BF16)
    kn = jnp.dot(cn, w_uk_ref[...], preferred_element_type=F32)
    for hd in range(H):
        kn_ref[0, hd] = kn[:, hd * DN:(hd + 1) * DN].astype(BF16)
    vT = lax.dot_general(w_uvT_ref[...], cn, _NT, preferred_element_type=F32)
    ones_row = (lax.broadcasted_iota(jnp.int32, (DVE - DV, T), 0) == 0).astype(BF16)
    for hd in range(H):
        vT_ref[0, hd, 0, 0:DV, :] = vT[hd * DV:(hd + 1) * DV].astype(BF16)
        vT_ref[0, hd, 0, DV:DVE, :] = ones_row
    krT = lax.dot_general(w_drT_ref[...], hb, _NT, preferred_element_type=F32)
    y1, y2 = rope_t(krT[:DR])
    krT = jnp.concatenate([y1, y2, jnp.zeros((LANES - DR, T), F32)], axis=0)
    kr_ref[0] = krT.T.astype(BF16)

    proj = jnp.dot(hb, w_in_ref[...], preferred_element_type=F32)
    z = proj[:, RQ:]
    sz_ref[0] = (z * _sigmoid(z)).astype(BF16)
    cqn = _rms_norm_rows(proj[:, :RQ], qg_ref[...]).astype(BF16)
    qT = lax.dot_general(w_uqT_ref[...], cqn, _NT, preferred_element_type=F32) * q_scale
    DQ = DN + DR
    for hd in range(H):
        base = hd * DQ
        qT_ref[0, hd, 0, 0:DN, :] = qT[base:base + DN].astype(BF16)
        y1, y2 = rope_t(qT[base + DN:base + DQ])
        qT_ref[0, hd, 0, DN:DN + DR // 2, :] = y1.astype(BF16)
        qT_ref[0, hd, 0, DN + DR // 2:DQ, :] = y2.astype(BF16)
        qT_ref[0, hd, 0, DQ:DQP, :] = jnp.zeros((DQP - DQ, T), BF16)


def _mla_proj(h, pos3, freqs, w_down_c, kv_norm_g, w_uk, w_uvT, w_drT, w_in, q_norm_g, w_uqT,
              *, H, DN, DR, DV, q_scale):
    B, S, D = h.shape
    T = T_P
    NT = S // T
    RKV = w_down_c.shape[1]
    RQ = q_norm_g.shape[-1]
    DQP = 2 * LANES
    DVE = DV + BF16_SUBLANES
    body = functools.partial(_mla_proj_body, H=H, RQ=RQ, DN=DN, DR=DR, DV=DV, DVE=DVE, DQP=DQP,
                             q_scale=q_scale)
    return pl.pallas_call(
        body,
        out_shape=(
            jax.ShapeDtypeStruct((B, H, NT, DQP, T), BF16),
            jax.ShapeDtypeStruct((B, S, H * DV), BF16),
            jax.ShapeDtypeStruct((B, H, S, DN), BF16),
            jax.ShapeDtypeStruct((B, S, LANES), BF16),
            jax.ShapeDtypeStruct((B, H, NT, DVE, T), BF16),
        ),
        grid=(B, NT),
        in_specs=[
            pl.BlockSpec((1, T, D), lambda b, t: (b, t, 0)),
            pl.BlockSpec((1, 1, T), lambda b, t: (b, 0, t)),
            _const_spec((DR // 2, 1)),
            _const_spec((D, RKV)), _const_spec((1, RKV)),
            _const_spec((RKV, H * DN)), _const_spec((H * DV, RKV)),
            _const_spec((LANES, D)),
            _const_spec((D, RQ + H * DV)), _const_spec((1, RQ)),
            _const_spec((H * (DN + DR), RQ)),
        ],
        out_specs=(
            pl.BlockSpec((1, H, 1, DQP, T), lambda b, t: (b, 0, t, 0, 0)),
            pl.BlockSpec((1, T, H * DV), lambda b, t: (b, t, 0)),
            pl.BlockSpec((1, H, T, DN), lambda b, t: (b, 0, t, 0)),
            pl.BlockSpec((1, T, LANES), lambda b, t: (b, t, 0)),
            pl.BlockSpec((1, H, 1, DVE, T), lambda b, t: (b, 0, t, 0, 0)),
        ),
        compiler_params=pltpu.CompilerParams(
            dimension_semantics=("arbitrary", "arbitrary"),
            vmem_limit_bytes=VMEM_LIMIT_BYTES),
        name="mla_proj",
    )(h, pos3, freqs, w_down_c, kv_norm_g, w_uk, w_uvT, w_drT, w_in, q_norm_g, w_uqT)


STEP_FIRST, STEP_PAD = 1, 2


def _attn_step_table(n_heads, nq):
    U = ATT_UNROLL
    lag = ATT_STAGES - 1
    rows = [(0, 0, 0, STEP_PAD, 0, 0)] * lag
    rows += [(qi, j, int(j == qi), STEP_FIRST if j == 0 else 0, (hd * nq + qi) % U, hd)
             for hd in range(n_heads) for qi in range(nq) for j in range(qi + 1)]
    n_iter = -(-len(rows) // U) * U
    rows += [(0, 0, 0, STEP_PAD, (n_heads * nq) % U, 0)] * (n_iter + lag - len(rows))
    for b in range(n_iter // U):
        done = set()
        for _, _, last, _, slot, _ in rows[b * U:(b + 1) * U]:
            assert slot not in done, "accumulator reused before its q block was stored"
            if last:
                done.add(slot)
    return np.asarray(rows, np.int32).T.copy(), n_iter


def _mla_attn_body(tab_ref, q_ref, kn_ref, kr_ref, v_ref, o_ref, s0, s1, bias, acc_sc, *,
                   TQ, TB, DV, N_ITER):
    R = TQ // TB
    sbuf = (s0, s1)

    @pl.when((pl.program_id(0) == 0) & (pl.program_id(1) == 0))
    def _():
        for ref in sbuf + (acc_sc,):
            ref[...] = jnp.zeros(ref.shape, ref.dtype)
        kpos = lax.broadcasted_iota(jnp.int32, (TQ, TQ), 0)
        qpos = lax.broadcasted_iota(jnp.int32, (TQ, TQ), 1)
        bias[0] = jnp.full((TQ, TQ), jnp.inf, F32)
        bias[1] = jnp.where(kpos <= qpos, jnp.inf, MASK_VALUE).astype(F32)

    def scores(c, slot):
        qi, j, diag, hd = tab_ref[0, c], tab_ref[1, c], tab_ref[2, c], tab_ref[5, c]
        k0 = pl.multiple_of(j * TQ, TQ)
        k = jnp.concatenate([kn_ref[0, hd, pl.ds(k0, TQ), :], kr_ref[0, pl.ds(k0, TQ), :]], axis=1)
        qT = jnp.concatenate([q_ref[0, hd, qi * R + r] for r in range(R)], axis=1)
        sT = jnp.minimum(jnp.dot(k, qT, preferred_element_type=F32), bias[diag])
        sbuf[slot][...] = sT
        return jnp.max(sT, axis=0, keepdims=True)

    def softmax_values(c, slot, m, cmax):
        start = tab_ref[3, c]
        m = jnp.where(start != 0, MASK_VALUE, m)
        m_new = jnp.maximum(m, jnp.where(start == STEP_PAD, -MASK_VALUE, cmax))
        p = jnp.exp2(sbuf[slot][...] - m_new).astype(BF16)
        j, a, hd = tab_ref[1, c], tab_ref[4, c], tab_ref[5, c]
        vT = jnp.concatenate([v_ref[0, hd, j * R + r] for r in range(R)], axis=1)
        acc_sc[a] = jnp.exp2(m - m_new) * acc_sc[a] + jnp.dot(vT, p, preferred_element_type=F32)
        return m_new

    def finish(c):
        @pl.when(tab_ref[2, c] == 1)
        def _():
            q0 = pl.multiple_of(tab_ref[0, c] * TQ, TQ)
            acc = acc_sc[tab_ref[4, c]]
            o = acc[:DV] * (1.0 / acc[DV:DV + 1])
            o_ref[0, tab_ref[5, c], pl.ds(q0, TQ), :] = o.T.astype(o_ref.dtype)

    def unrolled_iterations(iu, carry):
        m, cmax = carry
        for u in range(ATT_UNROLL):
            i = ATT_UNROLL * iu + u
            cmax_next = scores(i + 1, (u + 1) % 2)
            m = softmax_values(i, u % 2, m, cmax)
            cmax = cmax_next
        for u in range(ATT_UNROLL):
            finish(ATT_UNROLL * iu + u)
        return m, cmax

    init = (jnp.full((1, TQ), MASK_VALUE, F32), jnp.zeros((1, TQ), F32))
    lax.fori_loop(0, N_ITER // ATT_UNROLL, unrolled_iterations, init)


def _mla_attn(qT, kn, kr, vT):
    B, H, NB, DQP, TB = qT.shape
    S, DN = kn.shape[2], kn.shape[3]
    DVE = vT.shape[3]
    DV = DVE - BF16_SUBLANES
    TQ, G = T_ATT, ATT_HEADS
    assert H % G == 0
    table, n_iter = _attn_step_table(G, S // TQ)
    body = functools.partial(_mla_attn_body, TQ=TQ, TB=TB, DV=DV, N_ITER=n_iter)
    grid_spec = pltpu.PrefetchScalarGridSpec(
        num_scalar_prefetch=1,
        grid=(B, H // G),
        in_specs=[
            pl.BlockSpec((1, G, NB, DQP, TB), lambda b, h, tab: (b, h, 0, 0, 0)),
            pl.BlockSpec((1, G, S, DN), lambda b, h, tab: (b, h, 0, 0)),
            pl.BlockSpec((1, S, LANES), lambda b, h, tab: (b, 0, 0)),
            pl.BlockSpec((1, G, NB, DVE, TB), lambda b, h, tab: (b, h, 0, 0, 0)),
        ],
        out_specs=pl.BlockSpec((1, G, S, DV), lambda b, h, tab: (b, h, 0, 0)),
        scratch_shapes=[
            pltpu.VMEM((TQ, TQ), F32), pltpu.VMEM((TQ, TQ), F32),
            pltpu.VMEM((2, TQ, TQ), F32),
            pltpu.VMEM((ATT_UNROLL, DVE, TQ), F32),
        ])
    return pl.pallas_call(
        body,
        out_shape=jax.ShapeDtypeStruct((B, H, S, DV), BF16),
        grid_spec=grid_spec,
        compiler_params=pltpu.CompilerParams(
            dimension_semantics=("arbitrary", "arbitrary"),
            vmem_limit_bytes=VMEM_LIMIT_BYTES),
        name="mla_attn",
    )(jnp.asarray(table), qT, kn, kr, vT)


def _mla_out_body(o_ref, sz_ref, h_ref, w_out_ref, lg_ref, lb_ref, out_ref, *, H, alpha):
    o = jnp.concatenate([o_ref[0, hd] for hd in range(H)], axis=1).astype(F32)
    a = (o * sz_ref[0].astype(F32)).astype(BF16)
    y = jnp.dot(a, w_out_ref[...], preferred_element_type=F32)
    out_ref[0] = _layer_norm_rows(alpha * h_ref[0] + y, lg_ref[...], lb_ref[...])


def _mla_out(o, sz, h, w_out, ln_g, ln_b, alpha):
    B, H, S, DV = o.shape
    D = h.shape[-1]
    T = T_O
    body = functools.partial(_mla_out_body, H=H, alpha=alpha)
    row = lambda v: v.reshape(1, -1).astype(F32)
    return pl.pallas_call(
        body,
        out_shape=jax.ShapeDtypeStruct((B, S, D), F32),
        grid=(B, S // T),
        in_specs=[
            pl.BlockSpec((1, H, T, DV), lambda b, t: (b, 0, t, 0)),
            pl.BlockSpec((1, T, H * DV), lambda b, t: (b, t, 0)),
            pl.BlockSpec((1, T, D), lambda b, t: (b, t, 0)),
            _const_spec((H * DV, D)), _const_spec((1, D)), _const_spec((1, D)),
        ],
        out_specs=pl.BlockSpec((1, T, D), lambda b, t: (b, t, 0)),
        compiler_params=pltpu.CompilerParams(
            dimension_semantics=("arbitrary", "arbitrary"),
            vmem_limit_bytes=VMEM_LIMIT_BYTES),
        name="mla_out",
    )(o, sz, h, w_out.astype(BF16), row(ln_g), row(ln_b))


def kernel(x, positions, ln_g, ln_b, a_w_in, a_b_in, a_conv_w, a_conv_b, a_norm_g, a_norm_b,
           a_w_out, a_b_out, kv_w_down, kv_norm_g, kv_w_uk, kv_w_uv, b_w_in, b_q_norm_g, b_w_uq,
           b_w_out):
    B, S, D = x.shape
    depth = ln_g.shape[0]
    n_a, n_b = a_w_in.shape[0], b_w_in.shape[0]
    assert n_a + n_b == depth and n_b == 1, "one shared-KV attention layer is supported"
    alpha = (2.0 * depth) ** 0.25
    RKV, H, DN = kv_w_uk.shape
    DV = kv_w_uv.shape[2]
    DR = kv_w_down.shape[1] - RKV
    RQ = b_q_norm_g.shape[1]
    assert S % T_A == 0 and S % T_ATT == 0 and T_ATT % T_P == 0 and S % T_O == 0
    assert DR <= LANES and DN + DR <= 2 * LANES

    h = x
    for i in range(n_a):
        h = _layer_a(h, a_w_in[i], a_b_in[i], a_conv_w[i], a_conv_b[i], a_norm_g[i], a_norm_b[i],
                     a_w_out[i], a_b_out[i], ln_g[i], ln_b[i], alpha)

    freqs = (ROPE_THETA ** (-jnp.arange(0, DR, 2, dtype=F32) / DR)).reshape(DR // 2, 1)
    pos3 = positions.reshape(B, 1, S)
    w_down_c = kv_w_down[:, :RKV].astype(BF16)
    w_drT = jnp.zeros((LANES, D), BF16).at[:DR].set(kv_w_down[:, RKV:].T.astype(BF16))
    w_uk = kv_w_uk.reshape(RKV, H * DN).astype(BF16)
    w_uvT = kv_w_uv.reshape(RKV, H * DV).T.astype(BF16)
    q_scale = LOG2E / math.sqrt(DN + DR)

    j = 0
    layer = n_a + j
    w_uqT = b_w_uq[j].reshape(RQ, H * (DN + DR)).T.astype(BF16)
    qT, sz, kn, kr, vT = _mla_proj(
        h, pos3, freqs, w_down_c, kv_norm_g.reshape(1, RKV), w_uk, w_uvT, w_drT,
        b_w_in[j].astype(BF16), b_q_norm_g[j].reshape(1, RQ), w_uqT,
        H=H, DN=DN, DR=DR, DV=DV, q_scale=q_scale)
    o = _mla_attn(qT, kn, kr, vT)
    h = _mla_out(o, sz, h, b_w_out[j], ln_g[layer], ln_b[layer], alpha)
    return h
```

```python
import functools
import math

import jax
import jax.numpy as jnp
import numpy as np
from jax import lax
from jax.experimental import pallas as pl
from jax.experimental.pallas import tpu as pltpu

F32 = jnp.float32
BF16 = jnp.bfloat16

LN_EPS = 1e-5
RMS_EPS = 1e-6
ROPE_THETA = 10000.0
MASK_VALUE = -1e30
LOG2E = math.log2(math.e)

SUBLANES = 8
BF16_SUBLANES = 16
LANES = 128
VMEM_LIMIT_BYTES = 56 * 1024 * 1024

T_A = 256
CW_A = 512
U_PITCH = 2
T_P = 256
T_ATT = 512
ATT_UNROLL = 8
ATT_HEADS = 4
ATT_STAGES = 2

_NT = (((1,), (1,)), ((), ()))


def _sigmoid(x):
    return jax.nn.sigmoid(x)


def _layer_norm_rows(h, g, b):
    mu = jnp.mean(h, axis=-1, keepdims=True)
    d = h - mu
    var = jnp.mean(d * d, axis=-1, keepdims=True)
    return d * lax.rsqrt(var + LN_EPS) * g + b


def _rms_norm_rows(x, g):
    ms = jnp.mean(x * x, axis=-1, keepdims=True)
    return x * lax.rsqrt(ms + RMS_EPS) * g


def _const_spec(shape):
    n = len(shape)
    return pl.BlockSpec(shape, lambda *_: (0,) * n, pipeline_mode=pl.Buffered(1))


def _layer_a_body(x_ref, w_in_ref, b_in_ref, cw_ref, cb_ref, ng_ref, nb_ref, w_out_ref,
                  b_out_ref, lg_ref, lb_ref, o_ref, ubuf, cbuf, zbuf, *, T, E, CW, KW, HALO,
                  alpha):
    def usteps(start, size):
        return pl.ds(U_PITCH * start, size, stride=U_PITCH)

    @pl.when(pl.program_id(1) == 0)
    def _():
        ubuf[:, 0:U_PITCH * HALO, :] = jnp.zeros((E // LANES, U_PITCH * HALO, LANES), F32)

    x = x_ref[0]
    xb = x.astype(BF16)
    for c in range(E // CW):
        lo = c * CW

        def proj(off):
            w = w_in_ref[:, off + lo:off + lo + CW]
            return (jnp.dot(xb, w, preferred_element_type=F32)
                    + b_in_ref[:, off + lo:off + lo + CW])

        val, gate, z = proj(0), proj(E), proj(2 * E)
        u = val * _sigmoid(gate)
        zbuf[:, lo:lo + CW] = z * _sigmoid(z)
        for lt in range(CW // LANES):
            cs = slice(lo + lt * LANES, lo + (lt + 1) * LANES)
            s = lo // LANES + lt
            ubuf[s, usteps(HALO, T), :] = u[:, lt * LANES:(lt + 1) * LANES]
            acc = None
            for k in range(KW):
                term = ubuf[s, usteps(HALO - (KW - 1) + k, T), :] * cw_ref[k:k + 1, cs]
                acc = term if acc is None else acc + term
            cbuf[:, cs] = acc + cb_ref[:, cs]
            ubuf[s, usteps(0, HALO), :] = ubuf[s, usteps(T, HALO), :]

    g = _layer_norm_rows(cbuf[...], ng_ref[...], nb_ref[...])
    a = (g * _sigmoid(g) * zbuf[...]).astype(BF16)
    y = jnp.dot(a, w_out_ref[...], preferred_element_type=F32) + b_out_ref[...]
    o_ref[0] = _layer_norm_rows(alpha * x + y, lg_ref[...], lb_ref[...])


def _layer_a(h, w_in, b_in, conv_w, conv_b, norm_g, norm_b, w_out, b_out, ln_g, ln_b, alpha):
    B, S, D = h.shape
    E = w_out.shape[0]
    KW = conv_w.shape[0]
    HALO = -(-(KW - 1) // SUBLANES) * SUBLANES
    T, CW = T_A, CW_A
    body = functools.partial(_layer_a_body, T=T, E=E, CW=CW, KW=KW, HALO=HALO, alpha=alpha)
    row = lambda v: v.reshape(1, -1).astype(F32)
    return pl.pallas_call(
        body,
        out_shape=jax.ShapeDtypeStruct((B, S, D), F32),
        grid=(B, S // T),
        in_specs=[
            pl.BlockSpec((1, T, D), lambda b, t: (b, t, 0)),
            _const_spec((D, 3 * E)), _const_spec((1, 3 * E)),
            _const_spec((KW, E)), _const_spec((1, E)),
            _const_spec((1, E)), _const_spec((1, E)),
            _const_spec((E, D)), _const_spec((1, D)),
            _const_spec((1, D)), _const_spec((1, D)),
        ],
        out_specs=pl.BlockSpec((1, T, D), lambda b, t: (b, t, 0)),
        scratch_shapes=[
            pltpu.VMEM((E // LANES, U_PITCH * (HALO + T), LANES), F32),
            pltpu.VMEM((T, E), F32),
            pltpu.VMEM((T, E), F32),
        ],
        compiler_params=pltpu.CompilerParams(
            dimension_semantics=("arbitrary", "arbitrary"),
            vmem_limit_bytes=VMEM_LIMIT_BYTES),
        name="layer_a",
    )(h, w_in.astype(BF16), row(b_in), conv_w.astype(F32), row(conv_b), row(norm_g),
      row(norm_b), w_out.astype(BF16), row(b_out), row(ln_g), row(ln_b))


def _mla_proj_body(h_ref, pos_ref, fr_ref, w_down_ref, kvg_ref, w_uk_ref, w_uvT_ref, w_drT_ref,
                   w_in_ref, qg_ref, w_uqT_ref, qT_ref, sz_ref, kn_ref, kr_ref, vT_ref, *,
                   H, RQ, DN, DR, DV, DVE, DQP, q_scale):
    hb = h_ref[0].astype(BF16)
    T = hb.shape[0]
    ang = fr_ref[...] * pos_ref[0].astype(F32)
    cosT, sinT = jnp.cos(ang), jnp.sin(ang)

    def rope_t(xT):
        x1, x2 = xT[:DR // 2], xT[DR // 2:]
        return x1 * cosT - x2 * sinT, x2 * cosT + x1 * sinT

    ckv = jnp.dot(hb, w_down_ref[...], preferred_element_type=F32)
    cn = _rms_norm_rows(ckv, kvg_ref[...]).astype(BF16)
    kn = jnp.dot(cn, w_uk_ref[...], preferred_element_type=F32)
    for hd in range(H):
        kn_ref[0, hd] = kn[:, hd * DN:(hd + 1) * DN].astype(BF16)
    vT = lax.dot_general(w_uvT_ref[...], cn, _NT, preferred_element_type=F32)
    ones_row = (lax.broadcasted_iota(jnp.int32, (DVE - DV, T), 0) == 0).astype(BF16)
    for hd in range(H):
        vT_ref[0, hd, 0, 0:DV, :] = vT[hd * DV:(hd + 1) * DV].astype(BF16)
        vT_ref[0, hd, 0, DV:DVE, :] = ones_row
    krT = lax.dot_general(w_drT_ref[...], hb, _NT, preferred_element_type=F32)
    y1, y2 = rope_t(krT[:DR])
    krT = jnp.concatenate([y1, y2, jnp.zeros((LANES - DR, T), F32)], axis=0)
    kr_ref[0] = krT.T.astype(BF16)

    proj = jnp.dot(hb, w_in_ref[...], preferred_element_type=F32)
    z = proj[:, RQ:]
    sz_ref[0] = (z * _sigmoid(z)).astype(BF16)
    cqn = _rms_norm_rows(proj[:, :RQ], qg_ref[...]).astype(BF16)
    qT = lax.dot_general(w_uqT_ref[...], cqn, _NT, preferred_element_type=F32) * q_scale
    DQ = DN + DR
    for hd in range(H):
        base = hd * DQ
        qT_ref[0, hd, 0, 0:DN, :] = qT[base:base + DN].astype(BF16)
        y1, y2 = rope_t(qT[base + DN:base + DQ])
        qT_ref[0, hd, 0, DN:DN + DR // 2, :] = y1.astype(BF16)
        qT_ref[0, hd, 0, DN + DR // 2:DQ, :] = y2.astype(BF16)
        qT_ref[0, hd, 0, DQ:DQP, :] = jnp.zeros((DQP - DQ, T), BF16)


def _mla_proj(h, pos3, freqs, w_down_c, kv_norm_g, w_uk, w_uvT, w_drT, w_in, q_norm_g, w_uqT,
              *, H, DN, DR, DV, q_scale):
    B, S, D = h.shape
    T = T_P
    NT = S // T
    RKV = w_down_c.shape[1]
    RQ = q_norm_g.shape[-1]
    DQP = 2 * LANES
    DVE = DV + BF16_SUBLANES
    body = functools.partial(_mla_proj_body, H=H, RQ=RQ, DN=DN, DR=DR, DV=DV, DVE=DVE, DQP=DQP,
                             q_scale=q_scale)
    return pl.pallas_call(
        body,
        out_shape=(
            jax.ShapeDtypeStruct((B, H, NT, DQP, T), BF16),
            jax.ShapeDtypeStruct((B, S, H * DV), BF16),
            jax.ShapeDtypeStruct((B, H, S, DN), BF16),
            jax.ShapeDtypeStruct((B, S, LANES), BF16),
            jax.ShapeDtypeStruct((B, H, NT, DVE, T), BF16),
        ),
        grid=(B, NT),
        in_specs=[
            pl.BlockSpec((1, T, D), lambda b, t: (b, t, 0)),
            pl.BlockSpec((1, 1, T), lambda b, t: (b, 0, t)),
            _const_spec((DR // 2, 1)),
            _const_spec((D, RKV)), _const_spec((1, RKV)),
            _const_spec((RKV, H * DN)), _const_spec((H * DV, RKV)),
            _const_spec((LANES, D)),
            _const_spec((D, RQ + H * DV)), _const_spec((1, RQ)),
            _const_spec((H * (DN + DR), RQ)),
        ],
        out_specs=(
            pl.BlockSpec((1, H, 1, DQP, T), lambda b, t: (b, 0, t, 0, 0)),
            pl.BlockSpec((1, T, H * DV), lambda b, t: (b, t, 0)),
            pl.BlockSpec((1, H, T, DN), lambda b, t: (b, 0, t, 0)),
            pl.BlockSpec((1, T, LANES), lambda b, t: (b, t, 0)),
            pl.BlockSpec((1, H, 1, DVE, T), lambda b, t: (b, 0, t, 0, 0)),
        ),
        compiler_params=pltpu.CompilerParams(
            dimension_semantics=("arbitrary", "arbitrary"),
            vmem_limit_bytes=VMEM_LIMIT_BYTES),
        name="mla_proj",
    )(h, pos3, freqs, w_down_c, kv_norm_g, w_uk, w_uvT, w_drT, w_in, q_norm_g, w_uqT)


STEP_FIRST, STEP_PAD = 1, 2


def _attn_step_table(n_heads, nq):
    U = ATT_UNROLL
    lag = ATT_STAGES - 1
    rows = [(0, 0, 0, STEP_PAD, 0, 0)] * lag
    rows += [(qi, j, int(j == qi), STEP_FIRST if j == 0 else 0, (hd * nq + qi) % U, hd)
             for hd in range(n_heads) for qi in range(nq) for j in range(qi + 1)]
    n_iter = -(-len(rows) // U) * U
    rows += [(0, 0, 0, STEP_PAD, (n_heads * nq) % U, 0)] * (n_iter + lag - len(rows))
    for b in range(n_iter // U):
        done = set()
        for _, _, last, _, slot, _ in rows[b * U:(b + 1) * U]:
            assert slot not in done, "accumulator reused before its q block was stored"
            if last:
                done.add(slot)
    return np.asarray(rows, np.int32).T.copy(), n_iter


def _mla_attn_body(tab_ref, q_ref, kn_ref, kr_ref, v_ref, o_ref, s0, s1, bias, acc_sc, *,
                   TQ, TB, DV, N_ITER):
    R = TQ // TB
    sbuf = (s0, s1)

    @pl.when((pl.program_id(0) == 0) & (pl.program_id(1) == 0))
    def _():
        for ref in sbuf + (acc_sc,):
            ref[...] = jnp.zeros(ref.shape, ref.dtype)
        kpos = lax.broadcasted_iota(jnp.int32, (TQ, TQ), 0)
        qpos = lax.broadcasted_iota(jnp.int32, (TQ, TQ), 1)
        bias[0] = jnp.full((TQ, TQ), jnp.inf, F32)
        bias[1] = jnp.where(kpos <= qpos, jnp.inf, MASK_VALUE).astype(F32)

    def scores(c, slot):
        qi, j, diag, hd = tab_ref[0, c], tab_ref[1, c], tab_ref[2, c], tab_ref[5, c]
        k0 = pl.multiple_of(j * TQ, TQ)
        k = jnp.concatenate([kn_ref[0, hd, pl.ds(k0, TQ), :], kr_ref[0, pl.ds(k0, TQ), :]], axis=1)
        qT = jnp.concatenate([q_ref[0, hd, qi * R + r] for r in range(R)], axis=1)
        sT = jnp.minimum(jnp.dot(k, qT, preferred_element_type=F32), bias[diag])
        sbuf[slot][...] = sT
        return jnp.max(sT, axis=0, keepdims=True)

    def softmax_values(c, slot, m, cmax):
        start = tab_ref[3, c]
        m = jnp.where(start != 0, MASK_VALUE, m)
        m_new = jnp.maximum(m, jnp.where(start == STEP_PAD, -MASK_VALUE, cmax))
        p = jnp.exp2(sbuf[slot][...] - m_new).astype(BF16)
        j, a, hd = tab_ref[1, c], tab_ref[4, c], tab_ref[5, c]
        vT = jnp.concatenate([v_ref[0, hd, j * R + r] for r in range(R)], axis=1)
        acc_sc[a] = jnp.exp2(m - m_new) * acc_sc[a] + jnp.dot(vT, p, preferred_element_type=F32)
        return m_new

    def finish(c):
        @pl.when(tab_ref[2, c] == 1)
        def _():
            acc = acc_sc[tab_ref[4, c]]
            o = acc[:DV] * (1.0 / acc[DV:DV + 1])
            o_ref[0, tab_ref[5, c], tab_ref[0, c]] = o.astype(o_ref.dtype)

    def unrolled_iterations(iu, carry):
        m, cmax = carry
        for u in range(ATT_UNROLL):
            i = ATT_UNROLL * iu + u
            cmax_next = scores(i + 1, (u + 1) % 2)
            m = softmax_values(i, u % 2, m, cmax)
            cmax = cmax_next
        for u in range(ATT_UNROLL):
            finish(ATT_UNROLL * iu + u)
        return m, cmax

    init = (jnp.full((1, TQ), MASK_VALUE, F32), jnp.zeros((1, TQ), F32))
    lax.fori_loop(0, N_ITER // ATT_UNROLL, unrolled_iterations, init)


def _mla_attn(qT, kn, kr, vT):
    B, H, NB, DQP, TB = qT.shape
    S, DN = kn.shape[2], kn.shape[3]
    DVE = vT.shape[3]
    DV = DVE - BF16_SUBLANES
    TQ, G = T_ATT, ATT_HEADS
    assert H % G == 0
    table, n_iter = _attn_step_table(G, S // TQ)
    body = functools.partial(_mla_attn_body, TQ=TQ, TB=TB, DV=DV, N_ITER=n_iter)
    grid_spec = pltpu.PrefetchScalarGridSpec(
        num_scalar_prefetch=1,
        grid=(B, H // G),
        in_specs=[
            pl.BlockSpec((1, G, NB, DQP, TB), lambda b, h, tab: (b, h, 0, 0, 0)),
            pl.BlockSpec((1, G, S, DN), lambda b, h, tab: (b, h, 0, 0)),
            pl.BlockSpec((1, S, LANES), lambda b, h, tab: (b, 0, 0)),
            pl.BlockSpec((1, G, NB, DVE, TB), lambda b, h, tab: (b, h, 0, 0, 0)),
        ],
        out_specs=pl.BlockSpec((1, G, S // TQ, DV, TQ), lambda b, h, tab: (b, h, 0, 0, 0)),
        scratch_shapes=[
            pltpu.VMEM((TQ, TQ), F32), pltpu.VMEM((TQ, TQ), F32),
            pltpu.VMEM((2, TQ, TQ), F32),
            pltpu.VMEM((ATT_UNROLL, DVE, TQ), F32),
        ])
    return pl.pallas_call(
        body,
        out_shape=jax.ShapeDtypeStruct((B, H, S // TQ, DV, TQ), BF16),
        grid_spec=grid_spec,
        compiler_params=pltpu.CompilerParams(
            dimension_semantics=("arbitrary", "arbitrary"),
            vmem_limit_bytes=VMEM_LIMIT_BYTES),
        name="mla_attn",
    )(jnp.asarray(table), qT, kn, kr, vT)


def _mla_out_body(o_ref, sz_ref, h_ref, w_out_ref, lg_ref, lb_ref, out_ref, *, H, alpha):
    o = jnp.concatenate([o_ref[0, hd, 0].astype(F32).T for hd in range(H)], axis=1)
    a = (o * sz_ref[0].astype(F32)).astype(BF16)
    y = jnp.dot(a, w_out_ref[...], preferred_element_type=F32)
    out_ref[0] = _layer_norm_rows(alpha * h_ref[0] + y, lg_ref[...], lb_ref[...])


def _mla_out(o, sz, h, w_out, ln_g, ln_b, alpha):
    B, H, NQ, DV, T = o.shape
    S, D = h.shape[1], h.shape[2]
    body = functools.partial(_mla_out_body, H=H, alpha=alpha)
    row = lambda v: v.reshape(1, -1).astype(F32)
    return pl.pallas_call(
        body,
        out_shape=jax.ShapeDtypeStruct((B, S, D), F32),
        grid=(B, S // T),
        in_specs=[
            pl.BlockSpec((1, H, 1, DV, T), lambda b, t: (b, 0, t, 0, 0)),
            pl.BlockSpec((1, T, H * DV), lambda b, t: (b, t, 0)),
            pl.BlockSpec((1, T, D), lambda b, t: (b, t, 0)),
            _const_spec((H * DV, D)), _const_spec((1, D)), _const_spec((1, D)),
        ],
        out_specs=pl.BlockSpec((1, T, D), lambda b, t: (b, t, 0)),
        compiler_params=pltpu.CompilerParams(
            dimension_semantics=("arbitrary", "arbitrary"),
            vmem_limit_bytes=VMEM_LIMIT_BYTES),
        name="mla_out",
    )(o, sz, h, w_out.astype(BF16), row(ln_g), row(ln_b))


def kernel(x, positions, ln_g, ln_b, a_w_in, a_b_in, a_conv_w, a_conv_b, a_norm_g, a_norm_b,
           a_w_out, a_b_out, kv_w_down, kv_norm_g, kv_w_uk, kv_w_uv, b_w_in, b_q_norm_g, b_w_uq,
           b_w_out):
    B, S, D = x.shape
    depth = ln_g.shape[0]
    n_a, n_b = a_w_in.shape[0], b_w_in.shape[0]
    assert n_a + n_b == depth and n_b == 1, "one shared-KV attention layer is supported"
    alpha = (2.0 * depth) ** 0.25
    RKV, H, DN = kv_w_uk.shape
    DV = kv_w_uv.shape[2]
    DR = kv_w_down.shape[1] - RKV
    RQ = b_q_norm_g.shape[1]
    assert S % T_A == 0 and S % T_ATT == 0 and T_ATT % T_P == 0
    assert DR <= LANES and DN + DR <= 2 * LANES

    h = x
    for i in range(n_a):
        h = _layer_a(h, a_w_in[i], a_b_in[i], a_conv_w[i], a_conv_b[i], a_norm_g[i], a_norm_b[i],
                     a_w_out[i], a_b_out[i], ln_g[i], ln_b[i], alpha)

    freqs = (ROPE_THETA ** (-jnp.arange(0, DR, 2, dtype=F32) / DR)).reshape(DR // 2, 1)
    pos3 = positions.reshape(B, 1, S)
    w_down_c = kv_w_down[:, :RKV].astype(BF16)
    w_drT = jnp.zeros((LANES, D), BF16).at[:DR].set(kv_w_down[:, RKV:].T.astype(BF16))
    w_uk = kv_w_uk.reshape(RKV, H * DN).astype(BF16)
    w_uvT = kv_w_uv.reshape(RKV, H * DV).T.astype(BF16)
    q_scale = LOG2E / math.sqrt(DN + DR)

    j = 0
    layer = n_a + j
    w_uqT = b_w_uq[j].reshape(RQ, H * (DN + DR)).T.astype(BF16)
    qT, sz, kn, kr, vT = _mla_proj(
        h, pos3, freqs, w_down_c, kv_norm_g.reshape(1, RKV), w_uk, w_uvT, w_drT,
        b_w_in[j].astype(BF16), b_q_norm_g[j].reshape(1, RQ), w_uqT,
        H=H, DN=DN, DR=DR, DV=DV, q_scale=q_scale)
    o = _mla_attn(qT, kn, kr, vT)
    h = _mla_out(o, sz, h, b_w_out[j], ln_g[layer], ln_b[layer], alpha)
    return h
```

```python
import functools
import math

import jax
import jax.numpy as jnp
import numpy as np
from jax import lax
from jax.experimental import pallas as pl
from jax.experimental.pallas import tpu as pltpu

F32 = jnp.float32
BF16 = jnp.bfloat16

LN_EPS = 1e-5
RMS_EPS = 1e-6
ROPE_THETA = 10000.0
MASK_VALUE = -1e30
LOG2E = math.log2(math.e)

SUBLANES = 8
BF16_SUBLANES = 16
LANES = 128
VMEM_LIMIT_BYTES = 56 * 1024 * 1024

T_A = 256
CW_A = 512
U_PITCH = 2
T_P = 512
T_ATT = 512
ATT_UNROLL = 8
ATT_HEADS = 4

_NT = (((1,), (1,)), ((), ()))


def _sigmoid(x):
    return jax.nn.sigmoid(x)


def _layer_norm_rows(h, g, b):
    mu = jnp.mean(h, axis=-1, keepdims=True)
    d = h - mu
    var = jnp.mean(d * d, axis=-1, keepdims=True)
    return d * lax.rsqrt(var + LN_EPS) * g + b


def _rms_norm_rows(x, g):
    ms = jnp.mean(x * x, axis=-1, keepdims=True)
    return x * lax.rsqrt(ms + RMS_EPS) * g


def _const_spec(shape):
    n = len(shape)
    return pl.BlockSpec(shape, lambda *_: (0,) * n, pipeline_mode=pl.Buffered(1))


def _layer_a_body(x_ref, w_in_ref, b_in_ref, cw_ref, cb_ref, ng_ref, nb_ref, w_out_ref,
                  b_out_ref, lg_ref, lb_ref, o_ref, ubuf, cbuf, zbuf, *, T, E, CW, KW, HALO,
                  alpha):
    def usteps(start, size):
        return pl.ds(U_PITCH * start, size, stride=U_PITCH)

    @pl.when(pl.program_id(1) == 0)
    def _():
        ubuf[:, 0:U_PITCH * HALO, :] = jnp.zeros((E // LANES, U_PITCH * HALO, LANES), F32)

    x = x_ref[0]
    xb = x.astype(BF16)
    for c in range(E // CW):
        lo = c * CW

        def proj(off):
            w = w_in_ref[:, off + lo:off + lo + CW]
            return (jnp.dot(xb, w, preferred_element_type=F32)
                    + b_in_ref[:, off + lo:off + lo + CW])

        val, gate, z = proj(0), proj(E), proj(2 * E)
        u = val * _sigmoid(gate)
        zbuf[:, lo:lo + CW] = z * _sigmoid(z)
        for lt in range(CW // LANES):
            cs = slice(lo + lt * LANES, lo + (lt + 1) * LANES)
            s = lo // LANES + lt
            ubuf[s, usteps(HALO, T), :] = u[:, lt * LANES:(lt + 1) * LANES]
            acc = None
            for k in range(KW):
                term = ubuf[s, usteps(HALO - (KW - 1) + k, T), :] * cw_ref[k:k + 1, cs]
                acc = term if acc is None else acc + term
            cbuf[:, cs] = acc + cb_ref[:, cs]
            ubuf[s, usteps(0, HALO), :] = ubuf[s, usteps(T, HALO), :]

    g = _layer_norm_rows(cbuf[...], ng_ref[...], nb_ref[...])
    a = (g * _sigmoid(g) * zbuf[...]).astype(BF16)
    y = jnp.dot(a, w_out_ref[...], preferred_element_type=F32) + b_out_ref[...]
    o_ref[0] = _layer_norm_rows(alpha * x + y, lg_ref[...], lb_ref[...])


def _layer_a(h, w_in, b_in, conv_w, conv_b, norm_g, norm_b, w_out, b_out, ln_g, ln_b, alpha):
    B, S, D = h.shape
    E = w_out.shape[0]
    KW = conv_w.shape[0]
    HALO = -(-(KW - 1) // SUBLANES) * SUBLANES
    T, CW = T_A, CW_A
    body = functools.partial(_layer_a_body, T=T, E=E, CW=CW, KW=KW, HALO=HALO, alpha=alpha)
    row = lambda v: v.reshape(1, -1).astype(F32)
    return pl.pallas_call(
        body,
        out_shape=jax.ShapeDtypeStruct((B, S, D), F32),
        grid=(B, S // T),
        in_specs=[
            pl.BlockSpec((1, T, D), lambda b, t: (b, t, 0)),
            _const_spec((D, 3 * E)), _const_spec((1, 3 * E)),
            _const_spec((KW, E)), _const_spec((1, E)),
            _const_spec((1, E)), _const_spec((1, E)),
            _const_spec((E, D)), _const_spec((1, D)),
            _const_spec((1, D)), _const_spec((1, D)),
        ],
        out_specs=pl.BlockSpec((1, T, D), lambda b, t: (b, t, 0)),
        scratch_shapes=[
            pltpu.VMEM((E // LANES, U_PITCH * (HALO + T), LANES), F32),
            pltpu.VMEM((T, E), F32),
            pltpu.VMEM((T, E), F32),
        ],
        compiler_params=pltpu.CompilerParams(
            dimension_semantics=("arbitrary", "arbitrary"),
            vmem_limit_bytes=VMEM_LIMIT_BYTES),
        name="layer_a",
    )(h, w_in.astype(BF16), row(b_in), conv_w.astype(F32), row(conv_b), row(norm_g),
      row(norm_b), w_out.astype(BF16), row(b_out), row(ln_g), row(ln_b))


def _mla_proj_body(h_ref, pos_ref, fr_ref, w_down_ref, kvg_ref, w_uk_ref, w_uvT_ref, w_drT_ref,
                   w_in_ref, qg_ref, w_uqT_ref, qT_ref, sz_ref, kn_ref, kr_ref, vT_ref, *,
                   H, RQ, DN, DR, DV, DVE, DQP, q_scale):
    hb = h_ref[0].astype(BF16)
    T = hb.shape[0]
    ang = fr_ref[...] * pos_ref[0].astype(F32)
    cosT, sinT = jnp.cos(ang), jnp.sin(ang)

    def rope_t(xT):
        x1, x2 = xT[:DR // 2], xT[DR // 2:]
        return x1 * cosT - x2 * sinT, x2 * cosT + x1 * sinT

    ckv = jnp.dot(hb, w_down_ref[...], preferred_element_type=F32)
    cn = _rms_norm_rows(ckv, kvg_ref[...]).astype(BF16)
    kn = jnp.dot(cn, w_uk_ref[...], preferred_element_type=F32)
    for hd in range(H):
        kn_ref[0, hd] = kn[:, hd * DN:(hd + 1) * DN].astype(BF16)
    vT = lax.dot_general(w_uvT_ref[...], cn, _NT, preferred_element_type=F32)
    ones_row = (lax.broadcasted_iota(jnp.int32, (DVE - DV, T), 0) == 0).astype(BF16)
    for hd in range(H):
        vT_ref[0, hd, 0, 0:DV, :] = vT[hd * DV:(hd + 1) * DV].astype(BF16)
        vT_ref[0, hd, 0, DV:DVE, :] = ones_row
    krT = lax.dot_general(w_drT_ref[...], hb, _NT, preferred_element_type=F32)
    y1, y2 = rope_t(krT[:DR])
    krT = jnp.concatenate([y1, y2, jnp.zeros((LANES - DR, T), F32)], axis=0)
    kr_ref[0] = krT.T.astype(BF16)

    proj = jnp.dot(hb, w_in_ref[...], preferred_element_type=F32)
    z = proj[:, RQ:]
    sz_ref[0] = (z * _sigmoid(z)).astype(BF16)
    cqn = _rms_norm_rows(proj[:, :RQ], qg_ref[...]).astype(BF16)
    qT = lax.dot_general(w_uqT_ref[...], cqn, _NT, preferred_element_type=F32) * q_scale
    DQ = DN + DR
    for hd in range(H):
        base = hd * DQ
        qT_ref[0, hd, 0, 0:DN, :] = qT[base:base + DN].astype(BF16)
        y1, y2 = rope_t(qT[base + DN:base + DQ])
        qT_ref[0, hd, 0, DN:DN + DR // 2, :] = y1.astype(BF16)
        qT_ref[0, hd, 0, DN + DR // 2:DQ, :] = y2.astype(BF16)
        qT_ref[0, hd, 0, DQ:DQP, :] = jnp.zeros((DQP - DQ, T), BF16)


def _mla_proj(h, pos3, freqs, w_down_c, kv_norm_g, w_uk, w_uvT, w_drT, w_in, q_norm_g, w_uqT,
              *, H, DN, DR, DV, q_scale):
    B, S, D = h.shape
    T = T_P
    NT = S // T
    RKV = w_down_c.shape[1]
    RQ = q_norm_g.shape[-1]
    DQP = 2 * LANES
    DVE = DV + BF16_SUBLANES
    body = functools.partial(_mla_proj_body, H=H, RQ=RQ, DN=DN, DR=DR, DV=DV, DVE=DVE, DQP=DQP,
                             q_scale=q_scale)
    return pl.pallas_call(
        body,
        out_shape=(
            jax.ShapeDtypeStruct((B, H, NT, DQP, T), BF16),
            jax.ShapeDtypeStruct((B, S, H * DV), BF16),
            jax.ShapeDtypeStruct((B, H, S, DN), BF16),
            jax.ShapeDtypeStruct((B, S, LANES), BF16),
            jax.ShapeDtypeStruct((B, H, NT, DVE, T), BF16),
        ),
        grid=(B, NT),
        in_specs=[
            pl.BlockSpec((1, T, D), lambda b, t: (b, t, 0)),
            pl.BlockSpec((1, 1, T), lambda b, t: (b, 0, t)),
            _const_spec((DR // 2, 1)),
            _const_spec((D, RKV)), _const_spec((1, RKV)),
            _const_spec((RKV, H * DN)), _const_spec((H * DV, RKV)),
            _const_spec((LANES, D)),
            _const_spec((D, RQ + H * DV)), _const_spec((1, RQ)),
            _const_spec((H * (DN + DR), RQ)),
        ],
        out_specs=(
            pl.BlockSpec((1, H, 1, DQP, T), lambda b, t: (b, 0, t, 0, 0)),
            pl.BlockSpec((1, T, H * DV), lambda b, t: (b, t, 0)),
            pl.BlockSpec((1, H, T, DN), lambda b, t: (b, 0, t, 0)),
            pl.BlockSpec((1, T, LANES), lambda b, t: (b, t, 0)),
            pl.BlockSpec((1, H, 1, DVE, T), lambda b, t: (b, 0, t, 0, 0)),
        ),
        compiler_params=pltpu.CompilerParams(
            dimension_semantics=("arbitrary", "arbitrary"),
            vmem_limit_bytes=VMEM_LIMIT_BYTES),
        name="mla_proj",
    )(h, pos3, freqs, w_down_c, kv_norm_g, w_uk, w_uvT, w_drT, w_in, q_norm_g, w_uqT)


STEP_FIRST, STEP_PAD = 1, 2


def _attn_step_table(n_heads, nq):
    U = ATT_UNROLL
    rows = [(qi, j, int(j == qi), STEP_FIRST if j == 0 else 0, (hd * nq + qi) % U, hd)
            for hd in range(n_heads) for qi in range(nq) for j in range(qi + 1)]
    n_iter = -(-len(rows) // U) * U
    rows += [(0, 0, 0, STEP_PAD, (n_heads * nq) % U, 0)] * (n_iter + 1 - len(rows))
    for b in range(n_iter // U):
        done = set()
        for _, _, last, _, slot, _ in rows[b * U:(b + 1) * U]:
            assert slot not in done, "accumulator reused before its q block was stored"
            if last:
                done.add(slot)
    return np.asarray(rows, np.int32).T.copy(), n_iter


def _mla_attn_body(tab_ref, q_ref, kn_ref, kr_ref, v_ref, o_ref, s0, s1, bias, acc_sc, *,
                   TQ, TB, DV, N_ITER):
    R = TQ // TB
    sbuf = (s0, s1)

    @pl.when((pl.program_id(0) == 0) & (pl.program_id(1) == 0))
    def _():
        for ref in sbuf + (acc_sc,):
            ref[...] = jnp.zeros(ref.shape, ref.dtype)
        kpos = lax.broadcasted_iota(jnp.int32, (TQ, TQ), 0)
        qpos = lax.broadcasted_iota(jnp.int32, (TQ, TQ), 1)
        bias[0] = jnp.full((TQ, TQ), jnp.inf, F32)
        bias[1] = jnp.where(kpos <= qpos, jnp.inf, MASK_VALUE).astype(F32)

    def scores(c, slot):
        qi, j, diag, hd = tab_ref[0, c], tab_ref[1, c], tab_ref[2, c], tab_ref[5, c]
        k0 = pl.multiple_of(j * TQ, TQ)
        k = jnp.concatenate([kn_ref[0, hd, pl.ds(k0, TQ), :], kr_ref[0, pl.ds(k0, TQ), :]], axis=1)
        qT = jnp.concatenate([q_ref[0, hd, qi * R + r] for r in range(R)], axis=1)
        sT = jnp.minimum(jnp.dot(k, qT, preferred_element_type=F32), bias[diag])
        sbuf[slot][...] = sT
        return jnp.max(sT, axis=0, keepdims=True)

    def softmax_values(c, slot, m, cmax):
        start = tab_ref[3, c]
        m = jnp.where(start != 0, MASK_VALUE, m)
        m_new = jnp.maximum(m, jnp.where(start == STEP_PAD, -MASK_VALUE, cmax))
        p = jnp.exp2(sbuf[slot][...] - m_new).astype(BF16)
        j, a, hd = tab_ref[1, c], tab_ref[4, c], tab_ref[5, c]
        vT = jnp.concatenate([v_ref[0, hd, j * R + r] for r in range(R)], axis=1)
        acc_sc[a] = jnp.exp2(m - m_new) * acc_sc[a] + jnp.dot(vT, p, preferred_element_type=F32)
        return m_new

    def finish(c):
        @pl.when(tab_ref[2, c] == 1)
        def _():
            acc = acc_sc[tab_ref[4, c]]
            o = acc[:DV] * (1.0 / acc[DV:DV + 1])
            o_ref[0, tab_ref[5, c], tab_ref[0, c]] = o.astype(o_ref.dtype)

    def unrolled_iterations(iu, carry):
        m, cmax = carry
        for u in range(ATT_UNROLL):
            i = ATT_UNROLL * iu + u
            cmax_next = scores(i + 1, (u + 1) % 2)
            m = softmax_values(i, u % 2, m, cmax)
            cmax = cmax_next
        for u in range(ATT_UNROLL):
            finish(ATT_UNROLL * iu + u)
        return m, cmax

    init = (jnp.full((1, TQ), MASK_VALUE, F32), scores(0, 0))
    lax.fori_loop(0, N_ITER // ATT_UNROLL, unrolled_iterations, init)


def _mla_attn(qT, kn, kr, vT):
    B, H, NB, DQP, TB = qT.shape
    S, DN = kn.shape[2], kn.shape[3]
    DVE = vT.shape[3]
    DV = DVE - BF16_SUBLANES
    TQ, G = T_ATT, ATT_HEADS
    assert H % G == 0
    table, n_iter = _attn_step_table(G, S // TQ)
    body = functools.partial(_mla_attn_body, TQ=TQ, TB=TB, DV=DV, N_ITER=n_iter)
    grid_spec = pltpu.PrefetchScalarGridSpec(
        num_scalar_prefetch=1,
        grid=(B, H // G),
        in_specs=[
            pl.BlockSpec((1, G, NB, DQP, TB), lambda b, h, tab: (b, h, 0, 0, 0)),
            pl.BlockSpec((1, G, S, DN), lambda b, h, tab: (b, h, 0, 0)),
            pl.BlockSpec((1, S, LANES), lambda b, h, tab: (b, 0, 0)),
            pl.BlockSpec((1, G, NB, DVE, TB), lambda b, h, tab: (b, h, 0, 0, 0)),
        ],
        out_specs=pl.BlockSpec((1, G, S // TQ, DV, TQ), lambda b, h, tab: (b, h, 0, 0, 0)),
        scratch_shapes=[
            pltpu.VMEM((TQ, TQ), F32), pltpu.VMEM((TQ, TQ), F32),
            pltpu.VMEM((2, TQ, TQ), F32),
            pltpu.VMEM((ATT_UNROLL, DVE, TQ), F32),
        ])
    return pl.pallas_call(
        body,
        out_shape=jax.ShapeDtypeStruct((B, H, S // TQ, DV, TQ), BF16),
        grid_spec=grid_spec,
        compiler_params=pltpu.CompilerParams(
            dimension_semantics=("arbitrary", "arbitrary"),
            vmem_limit_bytes=VMEM_LIMIT_BYTES),
        name="mla_attn",
    )(jnp.asarray(table), qT, kn, kr, vT)


def _mla_out_body(o_ref, sz_ref, h_ref, w_out_ref, lg_ref, lb_ref, out_ref, *, H, alpha):
    o = jnp.concatenate([o_ref[0, hd, 0].astype(F32).T for hd in range(H)], axis=1)
    a = (o * sz_ref[0].astype(F32)).astype(BF16)
    y = jnp.dot(a, w_out_ref[...], preferred_element_type=F32)
    out_ref[0] = _layer_norm_rows(alpha * h_ref[0] + y, lg_ref[...], lb_ref[...])


def _mla_out(o, sz, h, w_out, ln_g, ln_b, alpha):
    B, H, NQ, DV, T = o.shape
    S, D = h.shape[1], h.shape[2]
    body = functools.partial(_mla_out_body, H=H, alpha=alpha)
    row = lambda v: v.reshape(1, -1).astype(F32)
    return pl.pallas_call(
        body,
        out_shape=jax.ShapeDtypeStruct((B, S, D), F32),
        grid=(B, S // T),
        in_specs=[
            pl.BlockSpec((1, H, 1, DV, T), lambda b, t: (b, 0, t, 0, 0)),
            pl.BlockSpec((1, T, H * DV), lambda b, t: (b, t, 0)),
            pl.BlockSpec((1, T, D), lambda b, t: (b, t, 0)),
            _const_spec((H * DV, D)), _const_spec((1, D)), _const_spec((1, D)),
        ],
        out_specs=pl.BlockSpec((1, T, D), lambda b, t: (b, t, 0)),
        compiler_params=pltpu.CompilerParams(
            dimension_semantics=("arbitrary", "arbitrary"),
            vmem_limit_bytes=VMEM_LIMIT_BYTES),
        name="mla_out",
    )(o, sz, h, w_out.astype(BF16), row(ln_g), row(ln_b))


def kernel(x, positions, ln_g, ln_b, a_w_in, a_b_in, a_conv_w, a_conv_b, a_norm_g, a_norm_b,
           a_w_out, a_b_out, kv_w_down, kv_norm_g, kv_w_uk, kv_w_uv, b_w_in, b_q_norm_g, b_w_uq,
           b_w_out):
    B, S, D = x.shape
    depth = ln_g.shape[0]
    n_a, n_b = a_w_in.shape[0], b_w_in.shape[0]
    assert n_a + n_b == depth and n_b == 1, "one shared-KV attention layer is supported"
    alpha = (2.0 * depth) ** 0.25
    RKV, H, DN = kv_w_uk.shape
    DV = kv_w_uv.shape[2]
    DR = kv_w_down.shape[1] - RKV
    RQ = b_q_norm_g.shape[1]
    assert S % T_A == 0 and S % T_ATT == 0 and T_ATT % T_P == 0
    assert DR <= LANES and DN + DR <= 2 * LANES

    h = x
    for i in range(n_a):
        h = _layer_a(h, a_w_in[i], a_b_in[i], a_conv_w[i], a_conv_b[i], a_norm_g[i], a_norm_b[i],
                     a_w_out[i], a_b_out[i], ln_g[i], ln_b[i], alpha)

    freqs = (ROPE_THETA ** (-jnp.arange(0, DR, 2, dtype=F32) / DR)).reshape(DR // 2, 1)
    pos3 = positions.reshape(B, 1, S)
    w_down_c = kv_w_down[:, :RKV].astype(BF16)
    w_drT = jnp.zeros((LANES, D), BF16).at[:DR].set(kv_w_down[:, RKV:].T.astype(BF16))
    w_uk = kv_w_uk.reshape(RKV, H * DN).astype(BF16)
    w_uvT = kv_w_uv.reshape(RKV, H * DV).T.astype(BF16)
    q_scale = LOG2E / math.sqrt(DN + DR)

    j = 0
    layer = n_a + j
    w_uqT = b_w_uq[j].reshape(RQ, H * (DN + DR)).T.astype(BF16)
    qT, sz, kn, kr, vT = _mla_proj(
        h, pos3, freqs, w_down_c, kv_norm_g.reshape(1, RKV), w_uk, w_uvT, w_drT,
        b_w_in[j].astype(BF16), b_q_norm_g[j].reshape(1, RQ), w_uqT,
        H=H, DN=DN, DR=DR, DV=DV, q_scale=q_scale)
    o = _mla_attn(qT, kn, kr, vT)
    h = _mla_out(o, sz, h, b_w_out[j], ln_g[layer], ln_b[layer], alpha)
    return h
```

```python
import functools
import math

import jax
import jax.numpy as jnp
import numpy as np
from jax import lax
from jax.experimental import pallas as pl
from jax.experimental.pallas import tpu as pltpu

F32 = jnp.float32
BF16 = jnp.bfloat16

LN_EPS = 1e-5
RMS_EPS = 1e-6
ROPE_THETA = 10000.0
MASK_VALUE = -1e30
LOG2E = math.log2(math.e)

SUBLANES = 8
BF16_SUBLANES = 16
LANES = 128
VMEM_LIMIT_BYTES = 56 * 1024 * 1024

T_A = 256
CW_A = 512
U_PITCH = 2
T_P = 512
T_ATT = 512
ATT_UNROLL = 12
ATT_HEADS = 4

_NT = (((1,), (1,)), ((), ()))


def _sigmoid(x):
    return jax.nn.sigmoid(x)


def _layer_norm_rows(h, g, b):
    mu = jnp.mean(h, axis=-1, keepdims=True)
    d = h - mu
    var = jnp.mean(d * d, axis=-1, keepdims=True)
    return d * lax.rsqrt(var + LN_EPS) * g + b


def _rms_norm_rows(x, g):
    ms = jnp.mean(x * x, axis=-1, keepdims=True)
    return x * lax.rsqrt(ms + RMS_EPS) * g


def _const_spec(shape):
    n = len(shape)
    return pl.BlockSpec(shape, lambda *_: (0,) * n, pipeline_mode=pl.Buffered(1))


def _layer_a_body(x_ref, w_in_ref, b_in_ref, cw_ref, cb_ref, ng_ref, nb_ref, w_out_ref,
                  b_out_ref, lg_ref, lb_ref, o_ref, ubuf, cbuf, zbuf, *, T, E, CW, KW, HALO,
                  alpha):
    def usteps(start, size):
        return pl.ds(U_PITCH * start, size, stride=U_PITCH)

    @pl.when(pl.program_id(1) == 0)
    def _():
        ubuf[:, 0:U_PITCH * HALO, :] = jnp.zeros((E // LANES, U_PITCH * HALO, LANES), F32)

    x = x_ref[0]
    xb = x.astype(BF16)
    for c in range(E // CW):
        lo = c * CW

        def proj(off):
            w = w_in_ref[:, off + lo:off + lo + CW]
            return (jnp.dot(xb, w, preferred_element_type=F32)
                    + b_in_ref[:, off + lo:off + lo + CW])

        val, gate, z = proj(0), proj(E), proj(2 * E)
        u = val * _sigmoid(gate)
        zbuf[:, lo:lo + CW] = z * _sigmoid(z)
        for lt in range(CW // LANES):
            cs = slice(lo + lt * LANES, lo + (lt + 1) * LANES)
            s = lo // LANES + lt
            ubuf[s, usteps(HALO, T), :] = u[:, lt * LANES:(lt + 1) * LANES]
            acc = None
            for k in range(KW):
                term = ubuf[s, usteps(HALO - (KW - 1) + k, T), :] * cw_ref[k:k + 1, cs]
                acc = term if acc is None else acc + term
            cbuf[:, cs] = acc + cb_ref[:, cs]
            ubuf[s, usteps(0, HALO), :] = ubuf[s, usteps(T, HALO), :]

    g = _layer_norm_rows(cbuf[...], ng_ref[...], nb_ref[...])
    a = (g * _sigmoid(g) * zbuf[...]).astype(BF16)
    y = jnp.dot(a, w_out_ref[...], preferred_element_type=F32) + b_out_ref[...]
    o_ref[0] = _layer_norm_rows(alpha * x + y, lg_ref[...], lb_ref[...])


def _layer_a(h, w_in, b_in, conv_w, conv_b, norm_g, norm_b, w_out, b_out, ln_g, ln_b, alpha):
    B, S, D = h.shape
    E = w_out.shape[0]
    KW = conv_w.shape[0]
    HALO = -(-(KW - 1) // SUBLANES) * SUBLANES
    T, CW = T_A, CW_A
    body = functools.partial(_layer_a_body, T=T, E=E, CW=CW, KW=KW, HALO=HALO, alpha=alpha)
    row = lambda v: v.reshape(1, -1).astype(F32)
    return pl.pallas_call(
        body,
        out_shape=jax.ShapeDtypeStruct((B, S, D), F32),
        grid=(B, S // T),
        in_specs=[
            pl.BlockSpec((1, T, D), lambda b, t: (b, t, 0)),
            _const_spec((D, 3 * E)), _const_spec((1, 3 * E)),
            _const_spec((KW, E)), _const_spec((1, E)),
            _const_spec((1, E)), _const_spec((1, E)),
            _const_spec((E, D)), _const_spec((1, D)),
            _const_spec((1, D)), _const_spec((1, D)),
        ],
        out_specs=pl.BlockSpec((1, T, D), lambda b, t: (b, t, 0)),
        scratch_shapes=[
            pltpu.VMEM((E // LANES, U_PITCH * (HALO + T), LANES), F32),
            pltpu.VMEM((T, E), F32),
            pltpu.VMEM((T, E), F32),
        ],
        compiler_params=pltpu.CompilerParams(
            dimension_semantics=("arbitrary", "arbitrary"),
            vmem_limit_bytes=VMEM_LIMIT_BYTES),
        name="layer_a",
    )(h, w_in.astype(BF16), row(b_in), conv_w.astype(F32), row(conv_b), row(norm_g),
      row(norm_b), w_out.astype(BF16), row(b_out), row(ln_g), row(ln_b))


def _mla_proj_body(h_ref, pos_ref, fr_ref, w_down_ref, kvg_ref, w_uk_ref, w_uvT_ref, w_drT_ref,
                   w_in_ref, qg_ref, w_uqT_ref, qT_ref, sz_ref, kn_ref, kr_ref, vT_ref, *,
                   H, RQ, DN, DR, DV, DVE, DQP, q_scale):
    hb = h_ref[0].astype(BF16)
    T = hb.shape[0]
    ang = fr_ref[...] * pos_ref[0].astype(F32)
    cosT, sinT = jnp.cos(ang), jnp.sin(ang)

    def rope_t(xT):
        x1, x2 = xT[:DR // 2], xT[DR // 2:]
        return x1 * cosT - x2 * sinT, x2 * cosT + x1 * sinT

    ckv = jnp.dot(hb, w_down_ref[...], preferred_element_type=F32)
    cn = _rms_norm_rows(ckv, kvg_ref[...]).astype(BF16)
    kn = jnp.dot(cn, w_uk_ref[...], preferred_element_type=F32)
    for hd in range(H):
        kn_ref[0, hd] = kn[:, hd * DN:(hd + 1) * DN].astype(BF16)
    vT = lax.dot_general(w_uvT_ref[...], cn, _NT, preferred_element_type=F32)
    ones_row = (lax.broadcasted_iota(jnp.int32, (DVE - DV, T), 0) == 0).astype(BF16)
    for hd in range(H):
        vT_ref[0, hd, 0, 0:DV, :] = vT[hd * DV:(hd + 1) * DV].astype(BF16)
        vT_ref[0, hd, 0, DV:DVE, :] = ones_row
    krT = lax.dot_general(w_drT_ref[...], hb, _NT, preferred_element_type=F32)
    y1, y2 = rope_t(krT[:DR])
    krT = jnp.concatenate([y1, y2, jnp.zeros((LANES - DR, T), F32)], axis=0)
    kr_ref[0] = krT.T.astype(BF16)

    proj = jnp.dot(hb, w_in_ref[...], preferred_element_type=F32)
    z = proj[:, RQ:]
    sz_ref[0] = (z * _sigmoid(z)).astype(BF16)
    cqn = _rms_norm_rows(proj[:, :RQ], qg_ref[...]).astype(BF16)
    qT = lax.dot_general(w_uqT_ref[...], cqn, _NT, preferred_element_type=F32) * q_scale
    DQ = DN + DR
    for hd in range(H):
        base = hd * DQ
        qT_ref[0, hd, 0, 0:DN, :] = qT[base:base + DN].astype(BF16)
        y1, y2 = rope_t(qT[base + DN:base + DQ])
        qT_ref[0, hd, 0, DN:DN + DR // 2, :] = y1.astype(BF16)
        qT_ref[0, hd, 0, DN + DR // 2:DQ, :] = y2.astype(BF16)
        qT_ref[0, hd, 0, DQ:DQP, :] = jnp.zeros((DQP - DQ, T), BF16)


def _mla_proj(h, pos3, freqs, w_down_c, kv_norm_g, w_uk, w_uvT, w_drT, w_in, q_norm_g, w_uqT,
              *, H, DN, DR, DV, q_scale):
    B, S, D = h.shape
    T = T_P
    NT = S // T
    RKV = w_down_c.shape[1]
    RQ = q_norm_g.shape[-1]
    DQP = 2 * LANES
    DVE = DV + BF16_SUBLANES
    body = functools.partial(_mla_proj_body, H=H, RQ=RQ, DN=DN, DR=DR, DV=DV, DVE=DVE, DQP=DQP,
                             q_scale=q_scale)
    return pl.pallas_call(
        body,
        out_shape=(
            jax.ShapeDtypeStruct((B, H, NT, DQP, T), BF16),
            jax.ShapeDtypeStruct((B, S, H * DV), BF16),
            jax.ShapeDtypeStruct((B, H, S, DN), BF16),
            jax.ShapeDtypeStruct((B, S, LANES), BF16),
            jax.ShapeDtypeStruct((B, H, NT, DVE, T), BF16),
        ),
        grid=(B, NT),
        in_specs=[
            pl.BlockSpec((1, T, D), lambda b, t: (b, t, 0)),
            pl.BlockSpec((1, 1, T), lambda b, t: (b, 0, t)),
            _const_spec((DR // 2, 1)),
            _const_spec((D, RKV)), _const_spec((1, RKV)),
            _const_spec((RKV, H * DN)), _const_spec((H * DV, RKV)),
            _const_spec((LANES, D)),
            _const_spec((D, RQ + H * DV)), _const_spec((1, RQ)),
            _const_spec((H * (DN + DR), RQ)),
        ],
        out_specs=(
            pl.BlockSpec((1, H, 1, DQP, T), lambda b, t: (b, 0, t, 0, 0)),
            pl.BlockSpec((1, T, H * DV), lambda b, t: (b, t, 0)),
            pl.BlockSpec((1, H, T, DN), lambda b, t: (b, 0, t, 0)),
            pl.BlockSpec((1, T, LANES), lambda b, t: (b, t, 0)),
            pl.BlockSpec((1, H, 1, DVE, T), lambda b, t: (b, 0, t, 0, 0)),
        ),
        compiler_params=pltpu.CompilerParams(
            dimension_semantics=("arbitrary", "arbitrary"),
            vmem_limit_bytes=VMEM_LIMIT_BYTES),
        name="mla_proj",
    )(h, pos3, freqs, w_down_c, kv_norm_g, w_uk, w_uvT, w_drT, w_in, q_norm_g, w_uqT)


STEP_FIRST, STEP_PAD = 1, 2


def _attn_step_table(n_heads, nq):
    U = ATT_UNROLL
    rows = [(qi, j, int(j == qi), STEP_FIRST if j == 0 else 0, (hd * nq + qi) % U, hd)
            for hd in range(n_heads) for qi in range(nq) for j in range(qi + 1)]
    n_iter = -(-len(rows) // U) * U
    rows += [(0, 0, 0, STEP_PAD, (n_heads * nq) % U, 0)] * (n_iter + 1 - len(rows))
    for b in range(n_iter // U):
        done = set()
        for _, _, last, _, slot, _ in rows[b * U:(b + 1) * U]:
            assert slot not in done, "accumulator reused before its q block was stored"
            if last:
                done.add(slot)
    return np.asarray(rows, np.int32).T.copy(), n_iter


def _mla_attn_body(tab_ref, q_ref, kn_ref, kr_ref, v_ref, o_ref, s0, s1, bias, acc_sc, *,
                   TQ, TB, DV, N_ITER):
    R = TQ // TB
    sbuf = (s0, s1)

    @pl.when((pl.program_id(0) == 0) & (pl.program_id(1) == 0))
    def _():
        for ref in sbuf + (acc_sc,):
            ref[...] = jnp.zeros(ref.shape, ref.dtype)
        kpos = lax.broadcasted_iota(jnp.int32, (TQ, TQ), 0)
        qpos = lax.broadcasted_iota(jnp.int32, (TQ, TQ), 1)
        bias[0] = jnp.full((TQ, TQ), jnp.inf, F32)
        bias[1] = jnp.where(kpos <= qpos, jnp.inf, MASK_VALUE).astype(F32)

    def scores(c, slot):
        qi, j, diag, hd = tab_ref[0, c], tab_ref[1, c], tab_ref[2, c], tab_ref[5, c]
        k0 = pl.multiple_of(j * TQ, TQ)
        k = jnp.concatenate([kn_ref[0, hd, pl.ds(k0, TQ), :], kr_ref[0, pl.ds(k0, TQ), :]], axis=1)
        qT = jnp.concatenate([q_ref[0, hd, qi * R + r] for r in range(R)], axis=1)
        sT = jnp.minimum(jnp.dot(k, qT, preferred_element_type=F32), bias[diag])
        sbuf[slot][...] = sT
        return jnp.max(sT, axis=0, keepdims=True)

    def softmax_values(c, slot, m, cmax):
        start = tab_ref[3, c]
        m = jnp.where(start != 0, MASK_VALUE, m)
        m_new = jnp.maximum(m, jnp.where(start == STEP_PAD, -MASK_VALUE, cmax))
        p = jnp.exp2(sbuf[slot][...] - m_new).astype(BF16)
        j, a, hd = tab_ref[1, c], tab_ref[4, c], tab_ref[5, c]
        vT = jnp.concatenate([v_ref[0, hd, j * R + r] for r in range(R)], axis=1)
        acc_sc[a] = jnp.exp2(m - m_new) * acc_sc[a] + jnp.dot(vT, p, preferred_element_type=F32)
        return m_new

    def finish(c):
        @pl.when(tab_ref[2, c] == 1)
        def _():
            acc = acc_sc[tab_ref[4, c]]
            o = acc[:DV] * (1.0 / acc[DV:DV + 1])
            o_ref[0, tab_ref[5, c], tab_ref[0, c]] = o.astype(o_ref.dtype)

    def unrolled_iterations(iu, carry):
        m, cmax = carry
        for u in range(ATT_UNROLL):
            i = ATT_UNROLL * iu + u
            cmax_next = scores(i + 1, (u + 1) % 2)
            m = softmax_values(i, u % 2, m, cmax)
            cmax = cmax_next
        for u in range(ATT_UNROLL):
            finish(ATT_UNROLL * iu + u)
        return m, cmax

    init = (jnp.full((1, TQ), MASK_VALUE, F32), scores(0, 0))
    lax.fori_loop(0, N_ITER // ATT_UNROLL, unrolled_iterations, init)


def _mla_attn(qT, kn, kr, vT):
    B, H, NB, DQP, TB = qT.shape
    S, DN = kn.shape[2], kn.shape[3]
    DVE = vT.shape[3]
    DV = DVE - BF16_SUBLANES
    TQ, G = T_ATT, ATT_HEADS
    assert H % G == 0
    table, n_iter = _attn_step_table(G, S // TQ)
    body = functools.partial(_mla_attn_body, TQ=TQ, TB=TB, DV=DV, N_ITER=n_iter)
    grid_spec = pltpu.PrefetchScalarGridSpec(
        num_scalar_prefetch=1,
        grid=(B, H // G),
        in_specs=[
            pl.BlockSpec((1, G, NB, DQP, TB), lambda b, h, tab: (b, h, 0, 0, 0)),
            pl.BlockSpec((1, G, S, DN), lambda b, h, tab: (b, h, 0, 0)),
            pl.BlockSpec((1, S, LANES), lambda b, h, tab: (b, 0, 0)),
            pl.BlockSpec((1, G, NB, DVE, TB), lambda b, h, tab: (b, h, 0, 0, 0)),
        ],
        out_specs=pl.BlockSpec((1, G, S // TQ, DV, TQ), lambda b, h, tab: (b, h, 0, 0, 0)),
        scratch_shapes=[
            pltpu.VMEM((TQ, TQ), F32), pltpu.VMEM((TQ, TQ), F32),
            pltpu.VMEM((2, TQ, TQ), F32),
            pltpu.VMEM((ATT_UNROLL, DVE, TQ), F32),
        ])
    return pl.pallas_call(
        body,
        out_shape=jax.ShapeDtypeStruct((B, H, S // TQ, DV, TQ), BF16),
        grid_spec=grid_spec,
        compiler_params=pltpu.CompilerParams(
            dimension_semantics=("arbitrary", "arbitrary"),
            vmem_limit_bytes=VMEM_LIMIT_BYTES),
        name="mla_attn",
    )(jnp.asarray(table), qT, kn, kr, vT)


def _mla_out_body(o_ref, sz_ref, h_ref, w_out_ref, lg_ref, lb_ref, out_ref, *, H, alpha):
    o = jnp.concatenate([o_ref[0, hd, 0].astype(F32).T for hd in range(H)], axis=1)
    a = (o * sz_ref[0].astype(F32)).astype(BF16)
    y = jnp.dot(a, w_out_ref[...], preferred_element_type=F32)
    out_ref[0] = _layer_norm_rows(alpha * h_ref[0] + y, lg_ref[...], lb_ref[...])


def _mla_out(o, sz, h, w_out, ln_g, ln_b, alpha):
    B, H, NQ, DV, T = o.shape
    S, D = h.shape[1], h.shape[2]
    body = functools.partial(_mla_out_body, H=H, alpha=alpha)
    row = lambda v: v.reshape(1, -1).astype(F32)
    return pl.pallas_call(
        body,
        out_shape=jax.ShapeDtypeStruct((B, S, D), F32),
        grid=(B, S // T),
        in_specs=[
            pl.BlockSpec((1, H, 1, DV, T), lambda b, t: (b, 0, t, 0, 0)),
            pl.BlockSpec((1, T, H * DV), lambda b, t: (b, t, 0)),
            pl.BlockSpec((1, T, D), lambda b, t: (b, t, 0)),
            _const_spec((H * DV, D)), _const_spec((1, D)), _const_spec((1, D)),
        ],
        out_specs=pl.BlockSpec((1, T, D), lambda b, t: (b, t, 0)),
        compiler_params=pltpu.CompilerParams(
            dimension_semantics=("arbitrary", "arbitrary"),
            vmem_limit_bytes=VMEM_LIMIT_BYTES),
        name="mla_out",
    )(o, sz, h, w_out.astype(BF16), row(ln_g), row(ln_b))


def kernel(x, positions, ln_g, ln_b, a_w_in, a_b_in, a_conv_w, a_conv_b, a_norm_g, a_norm_b,
           a_w_out, a_b_out, kv_w_down, kv_norm_g, kv_w_uk, kv_w_uv, b_w_in, b_q_norm_g, b_w_uq,
           b_w_out):
    B, S, D = x.shape
    depth = ln_g.shape[0]
    n_a, n_b = a_w_in.shape[0], b_w_in.shape[0]
    assert n_a + n_b == depth and n_b == 1, "one shared-KV attention layer is supported"
    alpha = (2.0 * depth) ** 0.25
    RKV, H, DN = kv_w_uk.shape
    DV = kv_w_uv.shape[2]
    DR = kv_w_down.shape[1] - RKV
    RQ = b_q_norm_g.shape[1]
    assert S % T_A == 0 and S % T_ATT == 0 and T_ATT % T_P == 0
    assert DR <= LANES and DN + DR <= 2 * LANES

    h = x
    for i in range(n_a):
        h = _layer_a(h, a_w_in[i], a_b_in[i], a_conv_w[i], a_conv_b[i], a_norm_g[i], a_norm_b[i],
                     a_w_out[i], a_b_out[i], ln_g[i], ln_b[i], alpha)

    freqs = (ROPE_THETA ** (-jnp.arange(0, DR, 2, dtype=F32) / DR)).reshape(DR // 2, 1)
    pos3 = positions.reshape(B, 1, S)
    w_down_c = kv_w_down[:, :RKV].astype(BF16)
    w_drT = jnp.zeros((LANES, D), BF16).at[:DR].set(kv_w_down[:, RKV:].T.astype(BF16))
    w_uk = kv_w_uk.reshape(RKV, H * DN).astype(BF16)
    w_uvT = kv_w_uv.reshape(RKV, H * DV).T.astype(BF16)
    q_scale = LOG2E / math.sqrt(DN + DR)

    j = 0
    layer = n_a + j
    w_uqT = b_w_uq[j].reshape(RQ, H * (DN + DR)).T.astype(BF16)
    qT, sz, kn, kr, vT = _mla_proj(
        h, pos3, freqs, w_down_c, kv_norm_g.reshape(1, RKV), w_uk, w_uvT, w_drT,
        b_w_in[j].astype(BF16), b_q_norm_g[j].reshape(1, RQ), w_uqT,
        H=H, DN=DN, DR=DR, DV=DV, q_scale=q_scale)
    o = _mla_attn(qT, kn, kr, vT)
    h = _mla_out(o, sz, h, b_w_out[j], ln_g[layer], ln_b[layer], alpha)
    return h
```

```python
import functools
import math

import jax
import jax.numpy as jnp
import numpy as np
from jax import lax
from jax.experimental import pallas as pl
from jax.experimental.pallas import tpu as pltpu

F32 = jnp.float32
BF16 = jnp.bfloat16

LN_EPS = 1e-5
RMS_EPS = 1e-6
ROPE_THETA = 10000.0
MASK_VALUE = -1e30
LOG2E = math.log2(math.e)

SUBLANES = 8
BF16_SUBLANES = 16
LANES = 128
VMEM_LIMIT_BYTES = 56 * 1024 * 1024

T_A = 256
CW_A = 512
U_PITCH = 2
T_P = 512
T_ATT = 512
ATT_UNROLL = 16
ATT_HEADS = 4

_NT = (((1,), (1,)), ((), ()))


def _sigmoid(x):
    return jax.nn.sigmoid(x)


def _layer_norm_rows(h, g, b):
    mu = jnp.mean(h, axis=-1, keepdims=True)
    d = h - mu
    var = jnp.mean(d * d, axis=-1, keepdims=True)
    return d * lax.rsqrt(var + LN_EPS) * g + b


def _rms_norm_rows(x, g):
    ms = jnp.mean(x * x, axis=-1, keepdims=True)
    return x * lax.rsqrt(ms + RMS_EPS) * g


def _const_spec(shape):
    n = len(shape)
    return pl.BlockSpec(shape, lambda *_: (0,) * n, pipeline_mode=pl.Buffered(1))


def _layer_a_body(x_ref, w_in_ref, b_in_ref, cw_ref, cb_ref, ng_ref, nb_ref, w_out_ref,
                  b_out_ref, lg_ref, lb_ref, o_ref, ubuf, cbuf, zbuf, *, T, E, CW, KW, HALO,
                  alpha):
    def usteps(start, size):
        return pl.ds(U_PITCH * start, size, stride=U_PITCH)

    @pl.when(pl.program_id(1) == 0)
    def _():
        ubuf[:, 0:U_PITCH * HALO, :] = jnp.zeros((E // LANES, U_PITCH * HALO, LANES), F32)

    x = x_ref[0]
    xb = x.astype(BF16)
    for c in range(E // CW):
        lo = c * CW

        def proj(off):
            w = w_in_ref[:, off + lo:off + lo + CW]
            return (jnp.dot(xb, w, preferred_element_type=F32)
                    + b_in_ref[:, off + lo:off + lo + CW])

        val, gate, z = proj(0), proj(E), proj(2 * E)
        u = val * _sigmoid(gate)
        zbuf[:, lo:lo + CW] = z * _sigmoid(z)
        for lt in range(CW // LANES):
            cs = slice(lo + lt * LANES, lo + (lt + 1) * LANES)
            s = lo // LANES + lt
            ubuf[s, usteps(HALO, T), :] = u[:, lt * LANES:(lt + 1) * LANES]
            acc = None
            for k in range(KW):
                term = ubuf[s, usteps(HALO - (KW - 1) + k, T), :] * cw_ref[k:k + 1, cs]
                acc = term if acc is None else acc + term
            cbuf[:, cs] = acc + cb_ref[:, cs]
            ubuf[s, usteps(0, HALO), :] = ubuf[s, usteps(T, HALO), :]

    g = _layer_norm_rows(cbuf[...], ng_ref[...], nb_ref[...])
    a = (g * _sigmoid(g) * zbuf[...]).astype(BF16)
    y = jnp.dot(a, w_out_ref[...], preferred_element_type=F32) + b_out_ref[...]
    o_ref[0] = _layer_norm_rows(alpha * x + y, lg_ref[...], lb_ref[...])


def _layer_a(h, w_in, b_in, conv_w, conv_b, norm_g, norm_b, w_out, b_out, ln_g, ln_b, alpha):
    B, S, D = h.shape
    E = w_out.shape[0]
    KW = conv_w.shape[0]
    HALO = -(-(KW - 1) // SUBLANES) * SUBLANES
    T, CW = T_A, CW_A
    body = functools.partial(_layer_a_body, T=T, E=E, CW=CW, KW=KW, HALO=HALO, alpha=alpha)
    row = lambda v: v.reshape(1, -1).astype(F32)
    return pl.pallas_call(
        body,
        out_shape=jax.ShapeDtypeStruct((B, S, D), F32),
        grid=(B, S // T),
        in_specs=[
            pl.BlockSpec((1, T, D), lambda b, t: (b, t, 0)),
            _const_spec((D, 3 * E)), _const_spec((1, 3 * E)),
            _const_spec((KW, E)), _const_spec((1, E)),
            _const_spec((1, E)), _const_spec((1, E)),
            _const_spec((E, D)), _const_spec((1, D)),
            _const_spec((1, D)), _const_spec((1, D)),
        ],
        out_specs=pl.BlockSpec((1, T, D), lambda b, t: (b, t, 0)),
        scratch_shapes=[
            pltpu.VMEM((E // LANES, U_PITCH * (HALO + T), LANES), F32),
            pltpu.VMEM((T, E), F32),
            pltpu.VMEM((T, E), F32),
        ],
        compiler_params=pltpu.CompilerParams(
            dimension_semantics=("arbitrary", "arbitrary"),
            vmem_limit_bytes=VMEM_LIMIT_BYTES),
        name="layer_a",
    )(h, w_in.astype(BF16), row(b_in), conv_w.astype(F32), row(conv_b), row(norm_g),
      row(norm_b), w_out.astype(BF16), row(b_out), row(ln_g), row(ln_b))


def _mla_proj_body(h_ref, pos_ref, fr_ref, w_down_ref, kvg_ref, w_uk_ref, w_uvT_ref, w_drT_ref,
                   w_in_ref, qg_ref, w_uqT_ref, qT_ref, sz_ref, kn_ref, kr_ref, vT_ref, *,
                   H, RQ, DN, DR, DV, DVE, DQP, q_scale):
    hb = h_ref[0].astype(BF16)
    T = hb.shape[0]
    ang = fr_ref[...] * pos_ref[0].astype(F32)
    cosT, sinT = jnp.cos(ang), jnp.sin(ang)

    def rope_t(xT):
        x1, x2 = xT[:DR // 2], xT[DR // 2:]
        return x1 * cosT - x2 * sinT, x2 * cosT + x1 * sinT

    ckv = jnp.dot(hb, w_down_ref[...], preferred_element_type=F32)
    cn = _rms_norm_rows(ckv, kvg_ref[...]).astype(BF16)
    kn = jnp.dot(cn, w_uk_ref[...], preferred_element_type=F32)
    for hd in range(H):
        kn_ref[0, hd] = kn[:, hd * DN:(hd + 1) * DN].astype(BF16)
    vT = lax.dot_general(w_uvT_ref[...], cn, _NT, preferred_element_type=F32)
    ones_row = (lax.broadcasted_iota(jnp.int32, (DVE - DV, T), 0) == 0).astype(BF16)
    for hd in range(H):
        vT_ref[0, hd, 0, 0:DV, :] = vT[hd * DV:(hd + 1) * DV].astype(BF16)
        vT_ref[0, hd, 0, DV:DVE, :] = ones_row
    krT = lax.dot_general(w_drT_ref[...], hb, _NT, preferred_element_type=F32)
    y1, y2 = rope_t(krT[:DR])
    krT = jnp.concatenate([y1, y2, jnp.zeros((LANES - DR, T), F32)], axis=0)
    kr_ref[0] = krT.T.astype(BF16)

    proj = jnp.dot(hb, w_in_ref[...], preferred_element_type=F32)
    z = proj[:, RQ:]
    sz_ref[0] = (z * _sigmoid(z)).astype(BF16)
    cqn = _rms_norm_rows(proj[:, :RQ], qg_ref[...]).astype(BF16)
    qT = lax.dot_general(w_uqT_ref[...], cqn, _NT, preferred_element_type=F32) * q_scale
    DQ = DN + DR
    for hd in range(H):
        base = hd * DQ
        qT_ref[0, hd, 0, 0:DN, :] = qT[base:base + DN].astype(BF16)
        y1, y2 = rope_t(qT[base + DN:base + DQ])
        qT_ref[0, hd, 0, DN:DN + DR // 2, :] = y1.astype(BF16)
        qT_ref[0, hd, 0, DN + DR // 2:DQ, :] = y2.astype(BF16)
        qT_ref[0, hd, 0, DQ:DQP, :] = jnp.zeros((DQP - DQ, T), BF16)


def _mla_proj(h, pos3, freqs, w_down_c, kv_norm_g, w_uk, w_uvT, w_drT, w_in, q_norm_g, w_uqT,
              *, H, DN, DR, DV, q_scale):
    B, S, D = h.shape
    T = T_P
    NT = S // T
    RKV = w_down_c.shape[1]
    RQ = q_norm_g.shape[-1]
    DQP = 2 * LANES
    DVE = DV + BF16_SUBLANES
    body = functools.partial(_mla_proj_body, H=H, RQ=RQ, DN=DN, DR=DR, DV=DV, DVE=DVE, DQP=DQP,
                             q_scale=q_scale)
    return pl.pallas_call(
        body,
        out_shape=(
            jax.ShapeDtypeStruct((B, H, NT, DQP, T), BF16),
            jax.ShapeDtypeStruct((B, S, H * DV), BF16),
            jax.ShapeDtypeStruct((B, H, S, DN), BF16),
            jax.ShapeDtypeStruct((B, S, LANES), BF16),
            jax.ShapeDtypeStruct((B, H, NT, DVE, T), BF16),
        ),
        grid=(B, NT),
        in_specs=[
            pl.BlockSpec((1, T, D), lambda b, t: (b, t, 0)),
            pl.BlockSpec((1, 1, T), lambda b, t: (b, 0, t)),
            _const_spec((DR // 2, 1)),
            _const_spec((D, RKV)), _const_spec((1, RKV)),
            _const_spec((RKV, H * DN)), _const_spec((H * DV, RKV)),
            _const_spec((LANES, D)),
            _const_spec((D, RQ + H * DV)), _const_spec((1, RQ)),
            _const_spec((H * (DN + DR), RQ)),
        ],
        out_specs=(
            pl.BlockSpec((1, H, 1, DQP, T), lambda b, t: (b, 0, t, 0, 0)),
            pl.BlockSpec((1, T, H * DV), lambda b, t: (b, t, 0)),
            pl.BlockSpec((1, H, T, DN), lambda b, t: (b, 0, t, 0)),
            pl.BlockSpec((1, T, LANES), lambda b, t: (b, t, 0)),
            pl.BlockSpec((1, H, 1, DVE, T), lambda b, t: (b, 0, t, 0, 0)),
        ),
        compiler_params=pltpu.CompilerParams(
            dimension_semantics=("arbitrary", "arbitrary"),
            vmem_limit_bytes=VMEM_LIMIT_BYTES),
        name="mla_proj",
    )(h, pos3, freqs, w_down_c, kv_norm_g, w_uk, w_uvT, w_drT, w_in, q_norm_g, w_uqT)


STEP_FIRST, STEP_PAD = 1, 2


def _attn_step_table(n_heads, nq):
    U = ATT_UNROLL
    rows = [(qi, j, int(j == qi), STEP_FIRST if j == 0 else 0, (hd * nq + qi) % U, hd)
            for hd in range(n_heads) for qi in range(nq) for j in range(qi + 1)]
    n_iter = -(-len(rows) // U) * U
    rows += [(0, 0, 0, STEP_PAD, (n_heads * nq) % U, 0)] * (n_iter + 1 - len(rows))
    for b in range(n_iter // U):
        done = set()
        for _, _, last, _, slot, _ in rows[b * U:(b + 1) * U]:
            assert slot not in done, "accumulator reused before its q block was stored"
            if last:
                done.add(slot)
    return np.asarray(rows, np.int32).T.copy(), n_iter


def _mla_attn_body(tab_ref, q_ref, kn_ref, kr_ref, v_ref, o_ref, s0, s1, bias, acc_sc, *,
                   TQ, TB, DV, N_ITER):
    R = TQ // TB
    sbuf = (s0, s1)

    @pl.when((pl.program_id(0) == 0) & (pl.program_id(1) == 0))
    def _():
        for ref in sbuf + (acc_sc,):
            ref[...] = jnp.zeros(ref.shape, ref.dtype)
        kpos = lax.broadcasted_iota(jnp.int32, (TQ, TQ), 0)
        qpos = lax.broadcasted_iota(jnp.int32, (TQ, TQ), 1)
        bias[0] = jnp.full((TQ, TQ), jnp.inf, F32)
        bias[1] = jnp.where(kpos <= qpos, jnp.inf, MASK_VALUE).astype(F32)

    def scores(c, slot):
        qi, j, diag, hd = tab_ref[0, c], tab_ref[1, c], tab_ref[2, c], tab_ref[5, c]
        k0 = pl.multiple_of(j * TQ, TQ)
        k = jnp.concatenate([kn_ref[0, hd, pl.ds(k0, TQ), :], kr_ref[0, pl.ds(k0, TQ), :]], axis=1)
        qT = jnp.concatenate([q_ref[0, hd, qi * R + r] for r in range(R)], axis=1)
        sT = jnp.minimum(jnp.dot(k, qT, preferred_element_type=F32), bias[diag])
        sbuf[slot][...] = sT
        return jnp.max(sT, axis=0, keepdims=True)

    def softmax_values(c, slot, m, cmax):
        start = tab_ref[3, c]
        m = jnp.where(start != 0, MASK_VALUE, m)
        m_new = jnp.maximum(m, jnp.where(start == STEP_PAD, -MASK_VALUE, cmax))
        p = jnp.exp2(sbuf[slot][...] - m_new).astype(BF16)
        j, a, hd = tab_ref[1, c], tab_ref[4, c], tab_ref[5, c]
        vT = jnp.concatenate([v_ref[0, hd, j * R + r] for r in range(R)], axis=1)
        acc_sc[a] = jnp.exp2(m - m_new) * acc_sc[a] + jnp.dot(vT, p, preferred_element_type=F32)
        return m_new

    def finish(c):
        @pl.when(tab_ref[2, c] == 1)
        def _():
            acc = acc_sc[tab_ref[4, c]]
            o = acc[:DV] * (1.0 / acc[DV:DV + 1])
            o_ref[0, tab_ref[5, c], tab_ref[0, c]] = o.astype(o_ref.dtype)

    def unrolled_iterations(iu, carry):
        m, cmax = carry
        for u in range(ATT_UNROLL):
            i = ATT_UNROLL * iu + u
            cmax_next = scores(i + 1, (u + 1) % 2)
            m = softmax_values(i, u % 2, m, cmax)
            cmax = cmax_next
        for u in range(ATT_UNROLL):
            finish(ATT_UNROLL * iu + u)
        return m, cmax

    init = (jnp.full((1, TQ), MASK_VALUE, F32), scores(0, 0))
    lax.fori_loop(0, N_ITER // ATT_UNROLL, unrolled_iterations, init)


def _mla_attn(qT, kn, kr, vT):
    B, H, NB, DQP, TB = qT.shape
    S, DN = kn.shape[2], kn.shape[3]
    DVE = vT.shape[3]
    DV = DVE - BF16_SUBLANES
    TQ, G = T_ATT, ATT_HEADS
    assert H % G == 0
    table, n_iter = _attn_step_table(G, S // TQ)
    body = functools.partial(_mla_attn_body, TQ=TQ, TB=TB, DV=DV, N_ITER=n_iter)
    grid_spec = pltpu.PrefetchScalarGridSpec(
        num_scalar_prefetch=1,
        grid=(B, H // G),
        in_specs=[
            pl.BlockSpec((1, G, NB, DQP, TB), lambda b, h, tab: (b, h, 0, 0, 0)),
            pl.BlockSpec((1, G, S, DN), lambda b, h, tab: (b, h, 0, 0)),
            pl.BlockSpec((1, S, LANES), lambda b, h, tab: (b, 0, 0)),
            pl.BlockSpec((1, G, NB, DVE, TB), lambda b, h, tab: (b, h, 0, 0, 0)),
        ],
        out_specs=pl.BlockSpec((1, G, S // TQ, DV, TQ), lambda b, h, tab: (b, h, 0, 0, 0)),
        scratch_shapes=[
            pltpu.VMEM((TQ, TQ), F32), pltpu.VMEM((TQ, TQ), F32),
            pltpu.VMEM((2, TQ, TQ), F32),
            pltpu.VMEM((ATT_UNROLL, DVE, TQ), F32),
        ])
    return pl.pallas_call(
        body,
        out_shape=jax.ShapeDtypeStruct((B, H, S // TQ, DV, TQ), BF16),
        grid_spec=grid_spec,
        compiler_params=pltpu.CompilerParams(
            dimension_semantics=("arbitrary", "arbitrary"),
            vmem_limit_bytes=VMEM_LIMIT_BYTES),
        name="mla_attn",
    )(jnp.asarray(table), qT, kn, kr, vT)


def _mla_out_body(o_ref, sz_ref, h_ref, w_out_ref, lg_ref, lb_ref, out_ref, *, H, alpha):
    o = jnp.concatenate([o_ref[0, hd, 0].astype(F32).T for hd in range(H)], axis=1)
    a = (o * sz_ref[0].astype(F32)).astype(BF16)
    y = jnp.dot(a, w_out_ref[...], preferred_element_type=F32)
    out_ref[0] = _layer_norm_rows(alpha * h_ref[0] + y, lg_ref[...], lb_ref[...])


def _mla_out(o, sz, h, w_out, ln_g, ln_b, alpha):
    B, H, NQ, DV, T = o.shape
    S, D = h.shape[1], h.shape[2]
    body = functools.partial(_mla_out_body, H=H, alpha=alpha)
    row = lambda v: v.reshape(1, -1).astype(F32)
    return pl.pallas_call(
        body,
        out_shape=jax.ShapeDtypeStruct((B, S, D), F32),
        grid=(B, S // T),
        in_specs=[
            pl.BlockSpec((1, H, 1, DV, T), lambda b, t: (b, 0, t, 0, 0)),
            pl.BlockSpec((1, T, H * DV), lambda b, t: (b, t, 0)),
            pl.BlockSpec((1, T, D), lambda b, t: (b, t, 0)),
            _const_spec((H * DV, D)), _const_spec((1, D)), _const_spec((1, D)),
        ],
        out_specs=pl.BlockSpec((1, T, D), lambda b, t: (b, t, 0)),
        compiler_params=pltpu.CompilerParams(
            dimension_semantics=("arbitrary", "arbitrary"),
            vmem_limit_bytes=VMEM_LIMIT_BYTES),
        name="mla_out",
    )(o, sz, h, w_out.astype(BF16), row(ln_g), row(ln_b))


def kernel(x, positions, ln_g, ln_b, a_w_in, a_b_in, a_conv_w, a_conv_b, a_norm_g, a_norm_b,
           a_w_out, a_b_out, kv_w_down, kv_norm_g, kv_w_uk, kv_w_uv, b_w_in, b_q_norm_g, b_w_uq,
           b_w_out):
    B, S, D = x.shape
    depth = ln_g.shape[0]
    n_a, n_b = a_w_in.shape[0], b_w_in.shape[0]
    assert n_a + n_b == depth and n_b == 1, "one shared-KV attention layer is supported"
    alpha = (2.0 * depth) ** 0.25
    RKV, H, DN = kv_w_uk.shape
    DV = kv_w_uv.shape[2]
    DR = kv_w_down.shape[1] - RKV
    RQ = b_q_norm_g.shape[1]
    assert S % T_A == 0 and S % T_ATT == 0 and T_ATT % T_P == 0
    assert DR <= LANES and DN + DR <= 2 * LANES

    h = x
    for i in range(n_a):
        h = _layer_a(h, a_w_in[i], a_b_in[i], a_conv_w[i], a_conv_b[i], a_norm_g[i], a_norm_b[i],
                     a_w_out[i], a_b_out[i], ln_g[i], ln_b[i], alpha)

    freqs = (ROPE_THETA ** (-jnp.arange(0, DR, 2, dtype=F32) / DR)).reshape(DR // 2, 1)
    pos3 = positions.reshape(B, 1, S)
    w_down_c = kv_w_down[:, :RKV].astype(BF16)
    w_drT = jnp.zeros((LANES, D), BF16).at[:DR].set(kv_w_down[:, RKV:].T.astype(BF16))
    w_uk = kv_w_uk.reshape(RKV, H * DN).astype(BF16)
    w_uvT = kv_w_uv.reshape(RKV, H * DV).T.astype(BF16)
    q_scale = LOG2E / math.sqrt(DN + DR)

    j = 0
    layer = n_a + j
    w_uqT = b_w_uq[j].reshape(RQ, H * (DN + DR)).T.astype(BF16)
    qT, sz, kn, kr, vT = _mla_proj(
        h, pos3, freqs, w_down_c, kv_norm_g.reshape(1, RKV), w_uk, w_uvT, w_drT,
        b_w_in[j].astype(BF16), b_q_norm_g[j].reshape(1, RQ), w_uqT,
        H=H, DN=DN, DR=DR, DV=DV, q_scale=q_scale)
    o = _mla_attn(qT, kn, kr, vT)
    h = _mla_out(o, sz, h, b_w_out[j], ln_g[layer], ln_b[layer], alpha)
    return h
```
